```python
import jax, jax.numpy as jnp
from jax import lax
import numpy as np

D_MODEL = 1024
BATCH = 2
SEQ = 8192
DEPTH = 4

GLA_HEADS = 4
GLA_DK = D_MODEL // 2
GLA_DV = D_MODEL
GLA_HK = GLA_DK // GLA_HEADS
GLA_HV = GLA_DV // GLA_HEADS
GLA_RANK = 16
GLA_TAU = 16.0
CHUNK = 64
POOL_WIDTH = D_MODEL // 2
POOL_WINDOWS = (2, 4, 8, 16)
POOL_GROUPS = 4
POOL_GW = POOL_WIDTH // POOL_GROUPS
D_FF = 2816
N_BRANCH = 2
N_MOD = 9
IN_SIZES = (GLA_DK, GLA_DK, GLA_DV, GLA_DV, GLA_RANK, POOL_WIDTH, D_MODEL, D_MODEL)
IN_WIDTH = sum(IN_SIZES)
IN_SPLITS = tuple(int(s) for s in np.cumsum(IN_SIZES)[:-1])
ALPHA = (2 * DEPTH) ** 0.25
BETA = (8 * DEPTH) ** -0.25
LN_EPS = 1e-5

kernel_name = "hybrid_gla_pool_macaron_deepnorm"


def layer_norm(x, g, b):
    xf = x.astype(jnp.float32)
    mu = jnp.mean(xf, axis=-1, keepdims=True)
    var = jnp.mean(jnp.square(xf - mu), axis=-1, keepdims=True)
    y = (xf - mu) * lax.rsqrt(var + LN_EPS)
    return (y * g.astype(jnp.float32) + b.astype(jnp.float32)).astype(x.dtype)


def modulate(x, shift, scale):
    return x * (1.0 + scale[:, None, :]) + shift[:, None, :]


def swiglu(h, w_up, w_down):
    gu = h @ w_up
    g, u = jnp.split(gu, 2, axis=-1)
    return (jax.nn.silu(g) * u) @ w_down


def gla_chunked(q, k, v, log_a):
    B, S, H, DK = q.shape
    DV = v.shape[-1]
    N = S // CHUNK

    def chunkify(t):
        return t.reshape(B, N, CHUNK, H, t.shape[-1]).transpose(1, 0, 3, 2, 4)

    q, k, v, log_a = chunkify(q), chunkify(k), chunkify(v), chunkify(log_a)
    b = jnp.cumsum(log_a, axis=-2)
    b_last = b[..., -1:, :]
    q_t = q * jnp.exp(b)
    k_t = k * jnp.exp(-b)
    k_end = k * jnp.exp(b_last - b)
    mask = jnp.tril(jnp.ones((CHUNK, CHUNK), dtype=bool))
    scores = jnp.einsum('nbhid,nbhjd->nbhij', q_t, k_t)
    scores = jnp.where(mask, scores, 0.0)
    o_intra = jnp.einsum('nbhij,nbhjv->nbhiv', scores, v)

    def step(state, inp):
        q_c, k_c, v_c, decay = inp
        o = jnp.einsum('bhid,bhdv->bhiv', q_c, state)
        state = state * decay[:, :, 0, :, None] + jnp.einsum('bhjd,bhjv->bhdv', k_c, v_c)
        return state, o

    s0 = jnp.zeros((B, H, DK, DV), jnp.float32)
    _, o_inter = lax.scan(step, s0, (q_t, k_end, v, jnp.exp(b_last)))
    o = o_intra + o_inter
    return o.transpose(1, 0, 3, 2, 4).reshape(B, S, H, DV)


def causal_multiscale_pool(u):
    B, S, _ = u.shape
    cs = jnp.concatenate([jnp.zeros((B, 1, POOL_WIDTH), u.dtype), jnp.cumsum(u, axis=1)], axis=1)
    pos = jnp.arange(S)
    outs = []
    for gi, w in enumerate(POOL_WINDOWS):
        lo_c, hi_c = gi * POOL_GW, (gi + 1) * POOL_GW
        c = cs[..., lo_c:hi_c]
        lo = jnp.maximum(pos + 1 - w, 0)
        window_sum = c[:, 1:] - c[:, lo]
        count = jnp.minimum(pos + 1, w).astype(u.dtype)
        outs.append(window_sum / count[None, :, None] - u[..., lo_c:hi_c])
    return jnp.stack(outs, axis=2)


def hybrid_mixer(h, w_in, w_alpha, b_alpha, gla_norm_g, w_proj_gla,
                 w_pool, pool_scale, w_proj_pool, w_out):
    B, S, _ = h.shape
    z = h @ w_in
    q, k, v, r, a_lr, u, g_gla, g_pool = jnp.split(z, IN_SPLITS, axis=-1)
    f32 = jnp.float32
    q = (q.astype(f32) * GLA_HK ** -0.5).reshape(B, S, GLA_HEADS, GLA_HK)
    k = k.astype(f32).reshape(B, S, GLA_HEADS, GLA_HK)
    v = v.astype(f32).reshape(B, S, GLA_HEADS, GLA_HV)
    log_a = jax.nn.log_sigmoid((a_lr @ w_alpha + b_alpha).astype(f32)) / GLA_TAU
    log_a = log_a.reshape(B, S, GLA_HEADS, GLA_HK)
    o = gla_chunked(q, k, v, log_a)
    mu = jnp.mean(o, axis=-1, keepdims=True)
    var = jnp.mean(jnp.square(o - mu), axis=-1, keepdims=True)
    o = (o - mu) * lax.rsqrt(var + LN_EPS) * gla_norm_g.astype(f32)
    o = o.reshape(B, S, GLA_DV).astype(h.dtype) * jax.nn.silu(r)
    y_gla = o @ w_proj_gla
    p = causal_multiscale_pool(u.astype(f32)).astype(h.dtype)
    p = jnp.einsum('bsgi,gio->bsgo', p, w_pool).reshape(B, S, POOL_WIDTH) * pool_scale
    y_pool = p @ w_proj_pool
    merged = jax.nn.sigmoid(g_gla) * y_gla + jax.nn.sigmoid(g_pool) * y_pool
    return merged @ w_out


def setup_inputs(seed: int = 0) -> dict:
    key = jax.random.key(seed)
    ks = jax.random.split(key, 24)
    L, D = DEPTH, D_MODEL

    def nrm(k, shape, fan_in, scale=1.0):
        return jax.random.normal(k, shape, jnp.float32) * (scale * fan_in ** -0.5)

    def near_one(k, shape):
        return 1.0 + 0.05 * jax.random.normal(k, shape, jnp.float32)

    def small(k, shape):
        return 0.01 * jax.random.normal(k, shape, jnp.float32)

    return {
        "x": jax.random.normal(ks[0], (BATCH, SEQ, D), jnp.float32),
        "c": jax.random.normal(ks[1], (BATCH, D), jnp.float32),
        "w_ada": nrm(ks[2], (L, D, N_MOD * D), D, 0.1),
        "b_ada": small(ks[3], (L, N_MOD * D)),
        "ffn1_up": nrm(ks[4], (L, D, 2 * D_FF), D),
        "ffn1_down": nrm(ks[5], (L, D_FF, D), D_FF, BETA),
        "w_in": nrm(ks[6], (L, D, IN_WIDTH), D),
        "w_alpha": nrm(ks[7], (L, GLA_RANK, GLA_DK), GLA_RANK),
        "b_alpha": small(ks[8], (L, GLA_DK)),
        "gla_norm_g": near_one(ks[9], (L, GLA_HEADS, GLA_HV)),
        "w_proj_gla": nrm(ks[10], (L, GLA_DV, D), GLA_DV, BETA),
        "w_pool": nrm(ks[11], (L, POOL_GROUPS, POOL_GW, POOL_GW), POOL_GW),
        "pool_scale": near_one(ks[12], (L, POOL_WIDTH)),
        "w_proj_pool": nrm(ks[13], (L, POOL_WIDTH, D), POOL_WIDTH, BETA),
        "w_out": nrm(ks[14], (L, D, D), D, BETA),
        "ffn2_up": nrm(ks[15], (L, D, 2 * D_FF), D),
        "ffn2_down": nrm(ks[16], (L, D_FF, D), D_FF, BETA),
        "ln_g": near_one(ks[17], (L, 3, D)),
        "ln_b": small(ks[18], (L, 3, D)),
    }


def reference(x, c, w_ada, b_ada, ffn1_up, ffn1_down, w_in, w_alpha, b_alpha,
              gla_norm_g, w_proj_gla, w_pool, pool_scale, w_proj_pool, w_out,
              ffn2_up, ffn2_down, ln_g, ln_b):
    c_act = jax.nn.silu(c)
    for l in range(DEPTH):
        mod = c_act @ w_ada[l] + b_ada[l]
        sh1, sc1, gt1, sh2, sc2, gt2, sh3, sc3, gt3 = jnp.split(mod, N_MOD, axis=-1)
        h = modulate(x, sh1, sc1)
        y = swiglu(h, ffn1_up[l], ffn1_down[l])
        x = layer_norm(ALPHA * x + 0.5 * (1.0 + gt1[:, None, :]) * y, ln_g[l, 0], ln_b[l, 0])
        h = modulate(x, sh2, sc2)
        y = hybrid_mixer(h, w_in[l], w_alpha[l], b_alpha[l], gla_norm_g[l], w_proj_gla[l],
                         w_pool[l], pool_scale[l], w_proj_pool[l], w_out[l])
        x = layer_norm(ALPHA * x + (1.0 + gt2[:, None, :]) * y, ln_g[l, 1], ln_b[l, 1])
        h = modulate(x, sh3, sc3)
        y = swiglu(h, ffn2_up[l], ffn2_down[l])
        x = layer_norm(ALPHA * x + 0.5 * (1.0 + gt3[:, None, :]) * y, ln_g[l, 2], ln_b[l, 2])
    return x
```

```python
import functools

import jax
import jax.numpy as jnp
from jax import lax
from jax.experimental import pallas as pl
from jax.experimental.pallas import tpu as pltpu

F32 = jnp.float32
BF16 = jnp.bfloat16

D_MODEL = 1024
DEPTH = 4
GLA_HEADS = 4
GLA_DK = 512
GLA_DV = 1024
GLA_HK = 128
GLA_HV = 256
GLA_RANK = 16
GLA_TAU = 16.0
CHUNK = 64
POOL_WIDTH = 512
POOL_WINDOWS = (2, 4, 8, 16)
POOL_GW = 128
POOL_HALO = 16
D_FF = 2816
N_MOD = 9
ALPHA = (2 * DEPTH) ** 0.25
LN_EPS = 1e-5
LANES = 128

FFN_TM = 1024
FFN_TF = 256
MIX_TS = 512
CUM_ROWS = 256
ADA_TN = 2304
VMEM_LIMIT = 56 * 1024 * 1024


def _layer_norm(t, g, b):
    mu = jnp.mean(t, axis=-1, keepdims=True)
    d = t - mu
    var = jnp.mean(d * d, axis=-1, keepdims=True)
    return d * lax.rsqrt(var + LN_EPS) * g + b


def _silu(t):
    return t * jax.nn.sigmoid(t)


def _ada_kernel(c_ref, w_ref, b_ref, o_ref):
    c_act = _silu(c_ref[...]).astype(BF16)
    w = w_ref[0].astype(BF16)
    o_ref[0] = jnp.dot(c_act, w, preferred_element_type=F32) + b_ref[0]


def _ada_call(c_pad, w_ada, b_ada):
    L, D, N = w_ada.shape
    rows = c_pad.shape[0]
    return pl.pallas_call(
        _ada_kernel,
        grid=(L, N // ADA_TN),
        in_specs=[
            pl.BlockSpec((rows, D), lambda l, n: (0, 0)),
            pl.BlockSpec((1, D, ADA_TN), lambda l, n: (l, 0, n)),
            pl.BlockSpec((1, 1, ADA_TN), lambda l, n: (l, 0, n)),
        ],
        out_specs=pl.BlockSpec((1, rows, ADA_TN), lambda l, n: (l, 0, n)),
        out_shape=jax.ShapeDtypeStruct((L, rows, N), F32),
        compiler_params=pltpu.CompilerParams(
            dimension_semantics=("arbitrary", "arbitrary"),
            vmem_limit_bytes=VMEM_LIMIT),
        name="ada_mod",
    )(c_pad, w_ada, b_ada.reshape(L, 1, N))


def _ffn_kernel(x_ref, mod_ref, wg_ref, wu_ref, wd_ref, lng_ref, lnb_ref,
                o_ref, h_ref, acc_ref, *, mod_base):
    j = pl.program_id(1)

    @pl.when(j == 0)
    def _():
        sh = mod_ref[0, mod_base:mod_base + 1, :]
        sc = mod_ref[0, mod_base + 1:mod_base + 2, :]
        h_ref[...] = (x_ref[...] * (1.0 + sc) + sh).astype(BF16)
        acc_ref[...] = jnp.zeros_like(acc_ref)

    h = h_ref[...]
    g = jnp.dot(h, wg_ref[...].astype(BF16), preferred_element_type=F32)
    u = jnp.dot(h, wu_ref[...].astype(BF16), preferred_element_type=F32)
    a = (_silu(g) * u).astype(BF16)
    acc_ref[...] += jnp.dot(a, wd_ref[...].astype(BF16), preferred_element_type=F32)

    @pl.when(j == pl.num_programs(1) - 1)
    def _():
        gt = mod_ref[0, mod_base + 2:mod_base + 3, :]
        t = ALPHA * x_ref[...] + (0.5 * (1.0 + gt)) * acc_ref[...]
        o_ref[...] = _layer_norm(t, lng_ref[...], lnb_ref[...])


def _ffn_call(x2, mod_l, w_up, w_down, ln_g, ln_b, *, mod_base, seq):
    M, D = x2.shape
    nf = D_FF // FFN_TF
    tiles_per_seq = seq // FFN_TM
    return pl.pallas_call(
        functools.partial(_ffn_kernel, mod_base=mod_base),
        grid=(M // FFN_TM, nf),
        in_specs=[
            pl.BlockSpec((FFN_TM, D), lambda i, j: (i, 0)),
            pl.BlockSpec((1, N_MOD, D), lambda i, j: (i // tiles_per_seq, 0, 0)),
            pl.BlockSpec((D, FFN_TF), lambda i, j: (0, j)),
            pl.BlockSpec((D, FFN_TF), lambda i, j: (0, j + nf)),
            pl.BlockSpec((FFN_TF, D), lambda i, j: (j, 0)),
            pl.BlockSpec((1, D), lambda i, j: (0, 0)),
            pl.BlockSpec((1, D), lambda i, j: (0, 0)),
        ],
        out_specs=pl.BlockSpec((FFN_TM, D), lambda i, j: (i, 0)),
        out_shape=jax.ShapeDtypeStruct((M, D), F32),
        scratch_shapes=[
            pltpu.VMEM((FFN_TM, D), BF16),
            pltpu.VMEM((FFN_TM, D), F32),
        ],
        compiler_params=pltpu.CompilerParams(
            dimension_semantics=("arbitrary", "arbitrary"),
            vmem_limit_bytes=VMEM_LIMIT),
        name="ffn",
    )(x2, mod_l, w_up, w_up, w_down, ln_g.reshape(1, D), ln_b.reshape(1, D))


def _split_bf16(t, parts):
    out = []
    r = t
    for _ in range(parts):
        p = r.astype(BF16)
        out.append(p)
        r = r - p.astype(F32)
    return out


def _mixer_kernel(x_ref, mod_ref, wq_ref, wk_ref, wv_ref, wr_ref, wa_ref, wu_ref,
                  wgg_ref, wgp_ref, walpha_ref, balpha_ref, gng_ref, wpg_ref,
                  wpool_ref, pscale_ref, wpp_ref, wout_ref, lng_ref, lnb_ref,
                  o_ref,
                  state_ref, uext_ref, qt_ref, kt_ref, kend_ref, v_ref, dec_ref, og_ref):
    s = pl.program_id(1)
    TS = MIX_TS

    @pl.when(s == 0)
    def _():
        state_ref[...] = jnp.zeros_like(state_ref)
        uext_ref[0:POOL_HALO, :] = jnp.zeros((POOL_HALO, POOL_WIDTH), F32)

    @pl.when(s > 0)
    def _():
        uext_ref[0:POOL_HALO, :] = uext_ref[TS:TS + POOL_HALO, :]

    x = x_ref[0]
    sh = mod_ref[0, 3:4, :]
    sc = mod_ref[0, 4:5, :]
    gt = mod_ref[0, 5:6, :]
    h = (x * (1.0 + sc) + sh).astype(BF16)

    def proj(w_ref):
        return jnp.dot(h, w_ref[...], preferred_element_type=F32)

    a_lr = proj(wa_ref).astype(BF16)
    pre = jnp.dot(a_lr, walpha_ref[...], preferred_element_type=F32) + balpha_ref[...]
    log_a = (jnp.minimum(pre, 0.0) - jnp.log1p(jnp.exp(-jnp.abs(pre)))) / GLA_TAU

    ri = lax.broadcasted_iota(jnp.int32, (CUM_ROWS, CUM_ROWS), 0)
    ci = lax.broadcasted_iota(jnp.int32, (CUM_ROWS, CUM_ROWS), 1)
    tri = jnp.where((ri // CHUNK == ci // CHUNK) & (ci <= ri), 1.0, 0.0).astype(BF16)
    b_parts = []
    for g0 in range(0, TS, CUM_ROWS):
        acc = None
        for part in _split_bf16(log_a[g0:g0 + CUM_ROWS], 3):
            t = jnp.dot(tri, part, preferred_element_type=F32)
            acc = t if acc is None else acc + t
        b_parts.append(acc)
    b = jnp.concatenate(b_parts, axis=0)

    q = proj(wq_ref) * (GLA_HK ** -0.5)
    k = proj(wk_ref)
    qt_ref[...] = (q * jnp.exp(b)).astype(BF16)
    kt_ref[...] = (k * jnp.exp(-b)).astype(BF16)
    for c in range(TS // CHUNK):
        r0 = c * CHUNK
        b_c = b[r0:r0 + CHUNK]
        b_last = b_c[CHUNK - 1:CHUNK, :]
        kend_ref[r0:r0 + CHUNK, :] = (k[r0:r0 + CHUNK] * jnp.exp(b_last - b_c)).astype(BF16)
        dec_ref[c:c + 1, :] = jnp.exp(b_last)
    v_ref[...] = proj(wv_ref).astype(BF16)

    row = lax.broadcasted_iota(jnp.int32, (CHUNK, CHUNK), 0)
    col = lax.broadcasted_iota(jnp.int32, (CHUNK, CHUNK), 1)
    causal = col <= row
    nt = (((1,), (1,)), ((), ()))
    tn = (((0,), (0,)), ((), ()))
    for c in range(TS // CHUNK):
        r0 = c * CHUNK
        for hd in range(GLA_HEADS):
            k0 = hd * GLA_HK
            v0 = hd * GLA_HV
            q_c = qt_ref[r0:r0 + CHUNK, k0:k0 + GLA_HK]
            k_c = kt_ref[r0:r0 + CHUNK, k0:k0 + GLA_HK]
            ke_c = kend_ref[r0:r0 + CHUNK, k0:k0 + GLA_HK]
            v_c = v_ref[r0:r0 + CHUNK, v0:v0 + GLA_HV]
            st = state_ref[hd]
            scores = lax.dot_general(q_c, k_c, nt, preferred_element_type=F32)
            scores = jnp.where(causal, scores, 0.0).astype(BF16)
            o_c = jnp.dot(scores, v_c, preferred_element_type=F32)
            o_c = o_c + lax.dot_general(q_c, st.astype(BF16), nt, preferred_element_type=F32)
            og_ref[r0:r0 + CHUNK, v0:v0 + GLA_HV] = o_c
            kv = lax.dot_general(v_c, ke_c, tn, preferred_element_type=F32)
            state_ref[hd] = st * dec_ref[c:c + 1, k0:k0 + GLA_HK] + kv

    r_gate = _silu(proj(wr_ref))
    o_parts = []
    for hd in range(GLA_HEADS):
        v0 = hd * GLA_HV
        o_h = og_ref[:, v0:v0 + GLA_HV]
        mu = jnp.mean(o_h, axis=-1, keepdims=True)
        dlt = o_h - mu
        var = jnp.mean(dlt * dlt, axis=-1, keepdims=True)
        o_parts.append(dlt * lax.rsqrt(var + LN_EPS) * gng_ref[:, v0:v0 + GLA_HV])
    o_n = (jnp.concatenate(o_parts, axis=-1) * r_gate).astype(BF16)
    y_gla = jnp.dot(o_n, wpg_ref[...], preferred_element_type=F32)

    u = proj(wu_ref)
    uext_ref[POOL_HALO:POOL_HALO + TS, :] = u
    pos = s * TS + lax.broadcasted_iota(jnp.int32, (TS, POOL_GW), 0)
    p_parts = []
    for gi, w in enumerate(POOL_WINDOWS):
        c0 = gi * POOL_GW
        wsum = u[:, c0:c0 + POOL_GW]
        for sft in range(1, w):
            wsum = wsum + uext_ref[POOL_HALO - sft:POOL_HALO - sft + TS, c0:c0 + POOL_GW]
        cnt = jnp.minimum(pos + 1, w).astype(F32)
        p_g = (wsum / cnt - u[:, c0:c0 + POOL_GW]).astype(BF16)
        p_parts.append(jnp.dot(p_g, wpool_ref[gi], preferred_element_type=F32))
    p = (jnp.concatenate(p_parts, axis=-1) * pscale_ref[...]).astype(BF16)
    y_pool = jnp.dot(p, wpp_ref[...], preferred_element_type=F32)

    merged = jax.nn.sigmoid(proj(wgg_ref)) * y_gla + jax.nn.sigmoid(proj(wgp_ref)) * y_pool
    y = jnp.dot(merged.astype(BF16), wout_ref[...], preferred_element_type=F32)
    t = ALPHA * x + (1.0 + gt) * y
    o_ref[0] = _layer_norm(t, lng_ref[...], lnb_ref[...])


def _mixer_call(x, mod_l, wts, ln_g, ln_b):
    B, S, D = x.shape
    TS = MIX_TS

    def const_spec(a):
        nd = a.ndim
        return pl.BlockSpec(a.shape, lambda b, s: (0,) * nd, pipeline_mode=pl.Buffered(1))

    operands = list(wts) + [ln_g.reshape(1, D), ln_b.reshape(1, D)]
    return pl.pallas_call(
        _mixer_kernel,
        grid=(B, S // TS),
        in_specs=[
            pl.BlockSpec((1, TS, D), lambda b, s: (b, s, 0)),
            pl.BlockSpec((1, N_MOD, D), lambda b, s: (b, 0, 0)),
        ] + [const_spec(a) for a in operands],
        out_specs=pl.BlockSpec((1, TS, D), lambda b, s: (b, s, 0)),
        out_shape=jax.ShapeDtypeStruct((B, S, D), F32),
        scratch_shapes=[
            pltpu.VMEM((GLA_HEADS, GLA_HV, GLA_HK), F32),
            pltpu.VMEM((POOL_HALO + TS, POOL_WIDTH), F32),
            pltpu.VMEM((TS, GLA_DK), BF16),
            pltpu.VMEM((TS, GLA_DK), BF16),
            pltpu.VMEM((TS, GLA_DK), BF16),
            pltpu.VMEM((TS, GLA_DV), BF16),
            pltpu.VMEM((TS // CHUNK, GLA_DK), F32),
            pltpu.VMEM((TS, GLA_DV), F32),
        ],
        compiler_params=pltpu.CompilerParams(
            dimension_semantics=("arbitrary", "arbitrary"),
            vmem_limit_bytes=VMEM_LIMIT),
        name="mixer",
    )(x, mod_l, *operands)


def _mixer_weights(w_in, w_alpha, b_alpha, gla_norm_g, w_proj_gla, w_pool, pool_scale,
                   w_proj_pool, w_out):
    o_q, o_k, o_v, o_r = 0, GLA_DK, 2 * GLA_DK, 2 * GLA_DK + GLA_DV
    o_a = o_r + GLA_DV
    o_u = o_a + GLA_RANK
    o_gg = o_u + POOL_WIDTH
    o_gp = o_gg + D_MODEL
    wb = w_in.astype(BF16)
    w_a = jnp.pad(wb[:, o_a:o_u], ((0, 0), (0, LANES - GLA_RANK)))
    w_al = jnp.pad(w_alpha.astype(BF16), ((0, LANES - GLA_RANK), (0, 0)))
    return (
        wb[:, o_q:o_k], wb[:, o_k:o_v], wb[:, o_v:o_r], wb[:, o_r:o_a], w_a,
        wb[:, o_u:o_gg], wb[:, o_gg:o_gp], wb[:, o_gp:],
        w_al, b_alpha.reshape(1, GLA_DK), gla_norm_g.reshape(1, GLA_DV),
        w_proj_gla.astype(BF16), w_pool.astype(BF16), pool_scale.reshape(1, POOL_WIDTH),
        w_proj_pool.astype(BF16), w_out.astype(BF16),
    )


def kernel(x, c, w_ada, b_ada, ffn1_up, ffn1_down, w_in, w_alpha, b_alpha, gla_norm_g,
           w_proj_gla, w_pool, pool_scale, w_proj_pool, w_out, ffn2_up, ffn2_down, ln_g, ln_b):
    B, S, D = x.shape
    L = w_ada.shape[0]
    c_pad = jnp.pad(c, ((0, 8 - B), (0, 0)))
    mod = _ada_call(c_pad, w_ada, b_ada)[:, :B].reshape(L, B, N_MOD, D)
    for l in range(L):
        x2 = _ffn_call(x.reshape(B * S, D), mod[l], ffn1_up[l], ffn1_down[l],
                       ln_g[l, 0], ln_b[l, 0], mod_base=0, seq=S)
        wts = _mixer_weights(w_in[l], w_alpha[l], b_alpha[l], gla_norm_g[l], w_proj_gla[l],
                             w_pool[l], pool_scale[l], w_proj_pool[l], w_out[l])
        x = _mixer_call(x2.reshape(B, S, D), mod[l], wts, ln_g[l, 1], ln_b[l, 1])
        x2 = _ffn_call(x.reshape(B * S, D), mod[l], ffn2_up[l], ffn2_down[l],
                       ln_g[l, 2], ln_b[l, 2], mod_base=6, seq=S)
        x = x2.reshape(B, S, D)
    return x
```

```python
import functools

import jax
import jax.numpy as jnp
from jax import lax
from jax.experimental import pallas as pl
from jax.experimental.pallas import tpu as pltpu

F32 = jnp.float32
BF16 = jnp.bfloat16

D_MODEL = 1024
DEPTH = 4
GLA_HEADS = 4
GLA_DK = 512
GLA_DV = 1024
GLA_HK = 128
GLA_HV = 256
GLA_RANK = 16
GLA_TAU = 16.0
CHUNK = 64
POOL_WIDTH = 512
POOL_WINDOWS = (2, 4, 8, 16)
POOL_GW = 128
POOL_HALO = 16
D_FF = 2816
N_MOD = 9
ALPHA = (2 * DEPTH) ** 0.25
LN_EPS = 1e-5
LANES = 128

FFN_TM = 512
FFN_TF = 256
MIX_TS = 512
CUM_ROWS = 256
ADA_TN = 2304
VMEM_LIMIT = 56 * 1024 * 1024

IN_Q = 0
IN_K = IN_Q + GLA_DK
IN_V = IN_K + GLA_DK
IN_R = IN_V + GLA_DV
IN_U = IN_R + GLA_DV
IN_GG = IN_U + POOL_WIDTH
IN_GP = IN_GG + D_MODEL
IN_A = IN_GP + D_MODEL
IN_END = IN_A + LANES


def _layer_norm(t, g, b):
    mu = jnp.mean(t, axis=-1, keepdims=True)
    d = t - mu
    var = jnp.mean(d * d, axis=-1, keepdims=True)
    return d * lax.rsqrt(var + LN_EPS) * g + b


def _silu(t):
    return t * jax.nn.sigmoid(t)


def _layer_spec(a, layer):
    nd = a.ndim
    return pl.BlockSpec((1,) + a.shape[1:], lambda *_: (layer,) + (0,) * (nd - 1),
                        pipeline_mode=pl.Buffered(1))


def _ada_kernel(c_ref, w_ref, b_ref, o_ref):
    c_act = _silu(c_ref[...]).astype(BF16)
    w = w_ref[0].astype(BF16)
    o_ref[0] = jnp.dot(c_act, w, preferred_element_type=F32) + b_ref[0]


def _ada_call(c_pad, w_ada, b_ada):
    L, D, N = w_ada.shape
    rows = c_pad.shape[0]
    return pl.pallas_call(
        _ada_kernel,
        grid=(L, N // ADA_TN),
        in_specs=[
            pl.BlockSpec((rows, D), lambda l, n: (0, 0)),
            pl.BlockSpec((1, D, ADA_TN), lambda l, n: (l, 0, n)),
            pl.BlockSpec((1, 1, ADA_TN), lambda l, n: (l, 0, n)),
        ],
        out_specs=pl.BlockSpec((1, rows, ADA_TN), lambda l, n: (l, 0, n)),
        out_shape=jax.ShapeDtypeStruct((L, rows, N), F32),
        compiler_params=pltpu.CompilerParams(
            dimension_semantics=("arbitrary", "arbitrary"),
            vmem_limit_bytes=VMEM_LIMIT),
        name="ada_mod",
    )(c_pad, w_ada, b_ada.reshape(L, 1, N))


def _ffn_kernel(x_ref, mod_ref, wup_ref, wdn_ref, lng_ref, lnb_ref, o_ref, a_ref,
                *, mod_base, ln_row):
    x = x_ref[...]
    sh = mod_ref[0, 0, mod_base:mod_base + 1, :]
    sc = mod_ref[0, 0, mod_base + 1:mod_base + 2, :]
    gt = mod_ref[0, 0, mod_base + 2:mod_base + 3, :]
    h = (x * (1.0 + sc) + sh).astype(BF16)
    for c0 in range(0, D_FF, FFN_TF):
        g = jnp.dot(h, wup_ref[0, :, c0:c0 + FFN_TF], preferred_element_type=F32)
        u = jnp.dot(h, wup_ref[0, :, D_FF + c0:D_FF + c0 + FFN_TF], preferred_element_type=F32)
        a_ref[:, c0:c0 + FFN_TF] = (_silu(g) * u).astype(BF16)
    y = jnp.dot(a_ref[...], wdn_ref[0], preferred_element_type=F32)
    t = ALPHA * x + (0.5 * (1.0 + gt)) * y
    o_ref[...] = _layer_norm(t, lng_ref[0, ln_row:ln_row + 1, :], lnb_ref[0, ln_row:ln_row + 1, :])


def _ffn_call(x2, mod, w_up, w_down, ln_g, ln_b, *, layer, mod_base, ln_row, seq):
    M, D = x2.shape
    tiles_per_seq = seq // FFN_TM
    return pl.pallas_call(
        functools.partial(_ffn_kernel, mod_base=mod_base, ln_row=ln_row),
        grid=(M // FFN_TM,),
        in_specs=[
            pl.BlockSpec((FFN_TM, D), lambda i: (i, 0)),
            pl.BlockSpec((1, 1, N_MOD, D), lambda i: (layer, i // tiles_per_seq, 0, 0)),
            _layer_spec(w_up, layer),
            _layer_spec(w_down, layer),
            _layer_spec(ln_g, layer),
            _layer_spec(ln_b, layer),
        ],
        out_specs=pl.BlockSpec((FFN_TM, D), lambda i: (i, 0)),
        out_shape=jax.ShapeDtypeStruct((M, D), F32),
        scratch_shapes=[pltpu.VMEM((FFN_TM, D_FF), BF16)],
        compiler_params=pltpu.CompilerParams(
            dimension_semantics=("arbitrary",),
            vmem_limit_bytes=VMEM_LIMIT),
        name="ffn",
    )(x2, mod, w_up, w_down, ln_g, ln_b)


def _split_bf16(t, parts):
    out = []
    r = t
    for _ in range(parts):
        p = r.astype(BF16)
        out.append(p)
        r = r - p.astype(F32)
    return out


def _mixer_kernel(x_ref, mod_ref, win_ref, walpha_ref, balpha_ref, gng_ref, wpg_ref,
                  wpool_ref, pscale_ref, wpp_ref, wout_ref, lng_ref, lnb_ref,
                  o_ref,
                  state_ref, uext_ref, qt_ref, kt_ref, kend_ref, v_ref, dec_ref, og_ref):
    s = pl.program_id(1)
    TS = MIX_TS

    @pl.when(s == 0)
    def _():
        state_ref[...] = jnp.zeros_like(state_ref)
        uext_ref[0:POOL_HALO, :] = jnp.zeros((POOL_HALO, POOL_WIDTH), F32)

    @pl.when(s > 0)
    def _():
        uext_ref[0:POOL_HALO, :] = uext_ref[TS:TS + POOL_HALO, :]

    x = x_ref[0]
    sh = mod_ref[0, 0, 3:4, :]
    sc = mod_ref[0, 0, 4:5, :]
    gt = mod_ref[0, 0, 5:6, :]
    h = (x * (1.0 + sc) + sh).astype(BF16)

    def proj(c0, c1):
        return jnp.dot(h, win_ref[0, :, c0:c1], preferred_element_type=F32)

    a_lr = proj(IN_A, IN_END).astype(BF16)
    pre = jnp.dot(a_lr, walpha_ref[0], preferred_element_type=F32) + balpha_ref[0]
    log_a = (jnp.minimum(pre, 0.0) - jnp.log1p(jnp.exp(-jnp.abs(pre)))) / GLA_TAU

    ri = lax.broadcasted_iota(jnp.int32, (CUM_ROWS, CUM_ROWS), 0)
    ci = lax.broadcasted_iota(jnp.int32, (CUM_ROWS, CUM_ROWS), 1)
    tri = jnp.where((ri // CHUNK == ci // CHUNK) & (ci <= ri), 1.0, 0.0).astype(BF16)
    b_parts = []
    for g0 in range(0, TS, CUM_ROWS):
        acc = None
        for part in _split_bf16(log_a[g0:g0 + CUM_ROWS], 3):
            t = jnp.dot(tri, part, preferred_element_type=F32)
            acc = t if acc is None else acc + t
        b_parts.append(acc)
    b = jnp.concatenate(b_parts, axis=0)

    q = proj(IN_Q, IN_K) * (GLA_HK ** -0.5)
    k = proj(IN_K, IN_V)
    qt_ref[...] = (q * jnp.exp(b)).astype(BF16)
    kt_ref[...] = (k * jnp.exp(-b)).astype(BF16)
    for c in range(TS // CHUNK):
        r0 = c * CHUNK
        b_c = b[r0:r0 + CHUNK]
        b_last = b_c[CHUNK - 1:CHUNK, :]
        kend_ref[r0:r0 + CHUNK, :] = (k[r0:r0 + CHUNK] * jnp.exp(b_last - b_c)).astype(BF16)
        dec_ref[c:c + 1, :] = jnp.exp(b_last)
    v_ref[...] = proj(IN_V, IN_R).astype(BF16)

    row = lax.broadcasted_iota(jnp.int32, (CHUNK, CHUNK), 0)
    col = lax.broadcasted_iota(jnp.int32, (CHUNK, CHUNK), 1)
    causal = col <= row
    nt = (((1,), (1,)), ((), ()))
    tn = (((0,), (0,)), ((), ()))
    for c in range(TS // CHUNK):
        r0 = c * CHUNK
        for hd in range(GLA_HEADS):
            k0 = hd * GLA_HK
            v0 = hd * GLA_HV
            q_c = qt_ref[r0:r0 + CHUNK, k0:k0 + GLA_HK]
            k_c = kt_ref[r0:r0 + CHUNK, k0:k0 + GLA_HK]
            ke_c = kend_ref[r0:r0 + CHUNK, k0:k0 + GLA_HK]
            v_c = v_ref[r0:r0 + CHUNK, v0:v0 + GLA_HV]
            st = state_ref[hd]
            scores = lax.dot_general(q_c, k_c, nt, preferred_element_type=F32)
            scores = jnp.where(causal, scores, 0.0).astype(BF16)
            o_c = jnp.dot(scores, v_c, preferred_element_type=F32)
            o_c = o_c + lax.dot_general(q_c, st.astype(BF16), nt, preferred_element_type=F32)
            og_ref[r0:r0 + CHUNK, v0:v0 + GLA_HV] = o_c
            kv = lax.dot_general(v_c, ke_c, tn, preferred_element_type=F32)
            state_ref[hd] = st * dec_ref[c:c + 1, k0:k0 + GLA_HK] + kv

    r_gate = _silu(proj(IN_R, IN_U))
    o_parts = []
    for hd in range(GLA_HEADS):
        v0 = hd * GLA_HV
        o_h = og_ref[:, v0:v0 + GLA_HV]
        mu = jnp.mean(o_h, axis=-1, keepdims=True)
        dlt = o_h - mu
        var = jnp.mean(dlt * dlt, axis=-1, keepdims=True)
        o_parts.append(dlt * lax.rsqrt(var + LN_EPS) * gng_ref[0, :, v0:v0 + GLA_HV])
    o_n = (jnp.concatenate(o_parts, axis=-1) * r_gate).astype(BF16)
    y_gla = jnp.dot(o_n, wpg_ref[0], preferred_element_type=F32)

    u = proj(IN_U, IN_GG)
    uext_ref[POOL_HALO:POOL_HALO + TS, :] = u
    pos = s * TS + lax.broadcasted_iota(jnp.int32, (TS, POOL_GW), 0)
    p_parts = []
    for gi, w in enumerate(POOL_WINDOWS):
        c0 = gi * POOL_GW
        wsum = u[:, c0:c0 + POOL_GW]
        for sft in range(1, w):
            wsum = wsum + uext_ref[POOL_HALO - sft:POOL_HALO - sft + TS, c0:c0 + POOL_GW]
        cnt = jnp.minimum(pos + 1, w).astype(F32)
        p_g = (wsum / cnt - u[:, c0:c0 + POOL_GW]).astype(BF16)
        p_parts.append(jnp.dot(p_g, wpool_ref[0, gi], preferred_element_type=F32))
    p = (jnp.concatenate(p_parts, axis=-1) * pscale_ref[0]).astype(BF16)
    y_pool = jnp.dot(p, wpp_ref[0], preferred_element_type=F32)

    merged = (jax.nn.sigmoid(proj(IN_GG, IN_GP)) * y_gla
              + jax.nn.sigmoid(proj(IN_GP, IN_A)) * y_pool)
    y = jnp.dot(merged.astype(BF16), wout_ref[0], preferred_element_type=F32)
    t = ALPHA * x + (1.0 + gt) * y
    o_ref[0] = _layer_norm(t, lng_ref[0, 1:2, :], lnb_ref[0, 1:2, :])


def _mixer_call(x, mod, wts, ln_g, ln_b, *, layer):
    B, S, D = x.shape
    TS = MIX_TS
    operands = list(wts) + [ln_g, ln_b]
    return pl.pallas_call(
        _mixer_kernel,
        grid=(B, S // TS),
        in_specs=[
            pl.BlockSpec((1, TS, D), lambda b, s: (b, s, 0)),
            pl.BlockSpec((1, 1, N_MOD, D), lambda b, s: (layer, b, 0, 0)),
        ] + [_layer_spec(a, layer) for a in operands],
        out_specs=pl.BlockSpec((1, TS, D), lambda b, s: (b, s, 0)),
        out_shape=jax.ShapeDtypeStruct((B, S, D), F32),
        scratch_shapes=[
            pltpu.VMEM((GLA_HEADS, GLA_HV, GLA_HK), F32),
            pltpu.VMEM((POOL_HALO + TS, POOL_WIDTH), F32),
            pltpu.VMEM((TS, GLA_DK), BF16),
            pltpu.VMEM((TS, GLA_DK), BF16),
            pltpu.VMEM((TS, GLA_DK), BF16),
            pltpu.VMEM((TS, GLA_DV), BF16),
            pltpu.VMEM((TS // CHUNK, GLA_DK), F32),
            pltpu.VMEM((TS, GLA_DV), F32),
        ],
        compiler_params=pltpu.CompilerParams(
            dimension_semantics=("arbitrary", "arbitrary"),
            vmem_limit_bytes=VMEM_LIMIT),
        name="mixer",
    )(x, mod, *operands)


def _mixer_weights(w_in, w_alpha, b_alpha, gla_norm_g, w_proj_gla, w_pool, pool_scale,
                   w_proj_pool, w_out):
    L = w_in.shape[0]
    a0 = 2 * GLA_DK + 2 * GLA_DV
    a1 = a0 + GLA_RANK
    w_in_r = jnp.concatenate(
        [w_in[:, :, :a0], w_in[:, :, a1:], w_in[:, :, a0:a1],
         jnp.zeros((L, D_MODEL, LANES - GLA_RANK), w_in.dtype)], axis=-1).astype(BF16)
    w_al = jnp.pad(w_alpha, ((0, 0), (0, LANES - GLA_RANK), (0, 0))).astype(BF16)
    return (
        w_in_r, w_al, b_alpha.reshape(L, 1, GLA_DK), gla_norm_g.reshape(L, 1, GLA_DV),
        w_proj_gla.astype(BF16), w_pool.astype(BF16), pool_scale.reshape(L, 1, POOL_WIDTH),
        w_proj_pool.astype(BF16), w_out.astype(BF16),
    )


def kernel(x, c, w_ada, b_ada, ffn1_up, ffn1_down, w_in, w_alpha, b_alpha, gla_norm_g,
           w_proj_gla, w_pool, pool_scale, w_proj_pool, w_out, ffn2_up, ffn2_down, ln_g, ln_b):
    B, S, D = x.shape
    L = w_ada.shape[0]
    c_pad = jnp.pad(c, ((0, 8 - B), (0, 0)))
    mod = _ada_call(c_pad, w_ada, b_ada)[:, :B].reshape(L, B, N_MOD, D)
    wts = _mixer_weights(w_in, w_alpha, b_alpha, gla_norm_g, w_proj_gla, w_pool, pool_scale,
                         w_proj_pool, w_out)
    up1, dn1 = ffn1_up.astype(BF16), ffn1_down.astype(BF16)
    up2, dn2 = ffn2_up.astype(BF16), ffn2_down.astype(BF16)
    x2 = x.reshape(B * S, D)
    for l in range(L):
        x2 = _ffn_call(x2, mod, up1, dn1, ln_g, ln_b, layer=l, mod_base=0, ln_row=0, seq=S)
        x3 = _mixer_call(x2.reshape(B, S, D), mod, wts, ln_g, ln_b, layer=l)
        x2 = _ffn_call(x3.reshape(B * S, D), mod, up2, dn2, ln_g, ln_b, layer=l, mod_base=6,
                       ln_row=2, seq=S)
    return x2.reshape(B, S, D)
```

```python
import functools

import jax
import jax.numpy as jnp
from jax import lax
from jax.experimental import pallas as pl
from jax.experimental.pallas import tpu as pltpu

F32 = jnp.float32
BF16 = jnp.bfloat16

D_MODEL = 1024
DEPTH = 4
GLA_HEADS = 4
GLA_DK = 512
GLA_DV = 1024
GLA_HK = 128
GLA_HV = 256
GLA_RANK = 16
GLA_TAU = 16.0
CHUNK = 64
POOL_WIDTH = 512
POOL_WINDOWS = (2, 4, 8, 16)
POOL_GW = 128
POOL_HALO = 16
D_FF = 2816
N_MOD = 9
ALPHA = (2 * DEPTH) ** 0.25
LN_EPS = 1e-5
LANES = 128

FFN_TM = 512
FFN_TF = 256
MIX_TS = 512
LN_ROWS = 64
CUM_ROWS = 256
ADA_TN = 2304
VMEM_LIMIT = 56 * 1024 * 1024

QKVR_Q = 0
QKVR_K = QKVR_Q + GLA_DK
QKVR_V = QKVR_K + GLA_DK
QKVR_R = QKVR_V + GLA_DV
UGG_U = 0
UGG_GG = UGG_U + POOL_WIDTH
UGG_GP = UGG_GG + D_MODEL
PIECE = 256
N_PIECES = (GLA_DV + POOL_WIDTH + D_MODEL) // PIECE
CUM_TERMS = 2


def _layer_norm(t, g, b):
    mu = jnp.mean(t, axis=-1, keepdims=True)
    d = t - mu
    var = jnp.mean(d * d, axis=-1, keepdims=True)
    return d * lax.rsqrt(var + LN_EPS) * g + b


def _silu(t):
    return t * jax.nn.sigmoid(t)


def _zero_bits_of(v):
    bits = lax.bitcast_convert_type(v, jnp.uint32)
    acc = None
    for r0 in range(0, v.shape[0], 8):
        for c0 in range(0, v.shape[1], LANES):
            tile = bits[r0:r0 + 8, c0:c0 + LANES]
            acc = tile if acc is None else acc | tile
    return lax.shift_right_logical(lax.shift_right_logical(acc, jnp.uint32(16)), jnp.uint32(16))


def _or_bits(v, zero_bits):
    reps = v.shape[0] // zero_bits.shape[0]
    z = jnp.concatenate([zero_bits] * reps, axis=0) if reps > 1 else zero_bits
    return lax.bitcast_convert_type(lax.bitcast_convert_type(v, jnp.uint32) | z, F32)


def _layer_spec(a, layer):
    nd = a.ndim
    return pl.BlockSpec((1,) + a.shape[1:], lambda *_: (layer,) + (0,) * (nd - 1),
                        pipeline_mode=pl.Buffered(1))


def _ada_kernel(c_ref, w_ref, b_ref, o_ref):
    c_act = _silu(c_ref[...]).astype(BF16)
    w = w_ref[0].astype(BF16)
    o_ref[0] = jnp.dot(c_act, w, preferred_element_type=F32) + b_ref[0]


def _ada_call(c_pad, w_ada, b_ada):
    L, D, N = w_ada.shape
    rows = c_pad.shape[0]
    return pl.pallas_call(
        _ada_kernel,
        grid=(L, N // ADA_TN),
        in_specs=[
            pl.BlockSpec((rows, D), lambda l, n: (0, 0)),
            pl.BlockSpec((1, D, ADA_TN), lambda l, n: (l, 0, n)),
            pl.BlockSpec((1, 1, ADA_TN), lambda l, n: (l, 0, n)),
        ],
        out_specs=pl.BlockSpec((1, rows, ADA_TN), lambda l, n: (l, 0, n)),
        out_shape=jax.ShapeDtypeStruct((L, rows, N), F32),
        compiler_params=pltpu.CompilerParams(
            dimension_semantics=("arbitrary", "arbitrary"),
            vmem_limit_bytes=VMEM_LIMIT),
        name="ada_mod",
    )(c_pad, w_ada, b_ada.reshape(L, 1, N))


def _ffn_kernel(xp_ref, x_ref, modp_ref, mod_ref, wup_ref, wdn_ref, lng_ref, lnb_ref,
                o_ref, a_ref, y_ref, *, mod_base, ln_row, n_tiles):
    i = pl.program_id(0)

    def post_norm(rb):
        rows = pl.ds(rb * LN_ROWS, LN_ROWS)
        gt = modp_ref[0, 0, mod_base + 2:mod_base + 3, :]
        t = ALPHA * xp_ref[rows, :] + (0.5 * (1.0 + gt)) * y_ref[rows, :]
        out = _layer_norm(t, lng_ref[0, ln_row:ln_row + 1, :], lnb_ref[0, ln_row:ln_row + 1, :])
        o_ref[rows, :] = out
        return out

    @pl.when(i == 0)
    def _():
        y_ref[...] = jnp.zeros_like(y_ref)

    @pl.when(i < n_tiles)
    def _():
        sh = mod_ref[0, 0, mod_base:mod_base + 1, :]
        sc = mod_ref[0, 0, mod_base + 1:mod_base + 2, :]
        h = (x_ref[...] * (1.0 + sc) + sh).astype(BF16)
        anchor = None
        for ci, c0 in enumerate(range(0, D_FF, FFN_TF)):
            g = jnp.dot(h, wup_ref[0, :, c0:c0 + FFN_TF], preferred_element_type=F32)
            u = jnp.dot(h, wup_ref[0, :, D_FF + c0:D_FF + c0 + FFN_TF], preferred_element_type=F32)
            act = _silu(g) * u
            a_ref[:, c0:c0 + FFN_TF] = act.astype(BF16)
            if anchor is not None:
                a_ref[0:16, c0:c0 + LANES] = _or_bits(act[0:16, 0:LANES], anchor).astype(BF16)
            anchor = _zero_bits_of(post_norm(ci)) if ci < FFN_TM // LN_ROWS else None
        y_ref[...] = jnp.dot(a_ref[...], wdn_ref[0], preferred_element_type=F32)

    @pl.when(i == n_tiles)
    def _():
        for rb in range(FFN_TM // LN_ROWS):
            post_norm(rb)


def _ffn_call(x2, mod, w_up, w_down, ln_g, ln_b, *, layer, mod_base, ln_row, seq):
    M, D = x2.shape
    n_tiles = M // FFN_TM
    tiles_per_seq = seq // FFN_TM
    assert D_FF // FFN_TF >= FFN_TM // LN_ROWS

    def cur(i):
        return jnp.minimum(i, n_tiles - 1)

    def prev(i):
        return jnp.maximum(i - 1, 0)

    return pl.pallas_call(
        functools.partial(_ffn_kernel, mod_base=mod_base, ln_row=ln_row, n_tiles=n_tiles),
        grid=(n_tiles + 1,),
        in_specs=[
            pl.BlockSpec((FFN_TM, D), lambda i: (prev(i), 0)),
            pl.BlockSpec((FFN_TM, D), lambda i: (cur(i), 0)),
            pl.BlockSpec((1, 1, N_MOD, D), lambda i: (layer, prev(i) // tiles_per_seq, 0, 0)),
            pl.BlockSpec((1, 1, N_MOD, D), lambda i: (layer, cur(i) // tiles_per_seq, 0, 0)),
            _layer_spec(w_up, layer),
            _layer_spec(w_down, layer),
            _layer_spec(ln_g, layer),
            _layer_spec(ln_b, layer),
        ],
        out_specs=pl.BlockSpec((FFN_TM, D), lambda i: (prev(i), 0)),
        out_shape=jax.ShapeDtypeStruct((M, D), F32),
        scratch_shapes=[
            pltpu.VMEM((FFN_TM, D_FF), BF16),
            pltpu.VMEM((FFN_TM, D), F32),
        ],
        compiler_params=pltpu.CompilerParams(
            dimension_semantics=("arbitrary",),
            vmem_limit_bytes=VMEM_LIMIT),
        name="ffn",
    )(x2, x2, mod, mod, w_up, w_down, ln_g, ln_b)


def _split_bf16(t, parts):
    out = []
    r = t
    for _ in range(parts):
        p = r.astype(BF16)
        out.append(p)
        r = r - p.astype(F32)
    return out


def _mixer_kernel(x_ref, mod_ref, wqkvr_ref, wugg_ref, wa_ref, walpha_ref, balpha_ref, gng_ref,
                  wpg_ref, wpool_ref, pscale_ref, wpp_ref, wout_ref, lng_ref, lnb_ref,
                  o_ref,
                  state_ref, uext_ref, qt_ref, kt_ref, kend_ref, v_ref, dect_ref, og_ref,
                  rg_ref, gg_ref, gp_ref):
    s = pl.program_id(1)
    TS = MIX_TS
    NCH = TS // CHUNK

    @pl.when(s == 0)
    def _():
        state_ref[...] = jnp.zeros_like(state_ref)
        uext_ref[0:POOL_HALO, :] = jnp.zeros((POOL_HALO, POOL_WIDTH), F32)

    @pl.when(s > 0)
    def _():
        uext_ref[0:POOL_HALO, :] = uext_ref[TS:TS + POOL_HALO, :]

    x = x_ref[0]
    sh = mod_ref[0, 0, 3:4, :]
    sc = mod_ref[0, 0, 4:5, :]
    gt = mod_ref[0, 0, 5:6, :]
    h = (x * (1.0 + sc) + sh).astype(BF16)

    def proj(w_ref, c0, c1):
        return jnp.dot(h, w_ref[0, :, c0:c1], preferred_element_type=F32)

    a_lr = jnp.dot(h, wa_ref[0], preferred_element_type=F32).astype(BF16)
    pre = jnp.dot(a_lr, walpha_ref[0], preferred_element_type=F32) + balpha_ref[0]
    log_a = (jnp.minimum(pre, 0.0) - jnp.log1p(jnp.exp(-jnp.abs(pre)))) / GLA_TAU

    v_ref[...] = proj(wqkvr_ref, QKVR_V, QKVR_R).astype(BF16)

    ri = lax.broadcasted_iota(jnp.int32, (CUM_ROWS, CUM_ROWS), 0)
    ci = lax.broadcasted_iota(jnp.int32, (CUM_ROWS, CUM_ROWS), 1)
    tri = jnp.where((ri // CHUNK == ci // CHUNK) & (ci <= ri), 1.0, 0.0).astype(BF16)
    b_parts = []
    for g0 in range(0, TS, CUM_ROWS):
        acc = None
        for part in _split_bf16(log_a[g0:g0 + CUM_ROWS], CUM_TERMS):
            t = jnp.dot(tri, part, preferred_element_type=F32)
            acc = t if acc is None else acc + t
        b_parts.append(acc)
    b = jnp.concatenate(b_parts, axis=0)

    q = proj(wqkvr_ref, QKVR_Q, QKVR_K) * (GLA_HK ** -0.5)
    k = proj(wqkvr_ref, QKVR_K, QKVR_V)
    qt_ref[...] = (q * jnp.exp(b)).astype(BF16)
    kt_ref[...] = (k * jnp.exp(-b)).astype(BF16)
    dec_rows = []
    for c in range(NCH):
        r0 = c * CHUNK
        b_c = b[r0:r0 + CHUNK]
        b_last = b_c[CHUNK - 1:CHUNK, :]
        kend_ref[r0:r0 + CHUNK, :] = (k[r0:r0 + CHUNK] * jnp.exp(b_last - b_c)).astype(BF16)
        dec_rows.append(jnp.exp(b_last))
    dec_all = jnp.concatenate(
        dec_rows + [jnp.zeros((LANES - NCH, GLA_DK), F32)], axis=0)
    dect_ref[...] = dec_all.T

    def r_piece(c0):
        rg_ref[:, c0:c0 + PIECE] = _silu(proj(wqkvr_ref, QKVR_R + c0, QKVR_R + c0 + PIECE))

    def u_piece(c0):
        uext_ref[POOL_HALO:POOL_HALO + TS, c0:c0 + PIECE] = proj(wugg_ref, UGG_U + c0, UGG_U + c0 + PIECE)

    def gp_piece(c0):
        gp_ref[:, c0:c0 + PIECE] = jax.nn.sigmoid(proj(wugg_ref, UGG_GP + c0, UGG_GP + c0 + PIECE))

    def gg_piece(c0):
        gg_ref[:, c0:c0 + PIECE] = jax.nn.sigmoid(proj(wugg_ref, UGG_GG + c0, UGG_GG + c0 + PIECE))

    pieces = ([functools.partial(r_piece, c0) for c0 in range(0, GLA_DV, PIECE)]
              + [functools.partial(u_piece, c0) for c0 in range(0, POOL_WIDTH, PIECE)]
              + [functools.partial(gp_piece, c0) for c0 in range(0, D_MODEL, PIECE)])

    row = lax.broadcasted_iota(jnp.int32, (CHUNK, CHUNK), 0)
    col = lax.broadcasted_iota(jnp.int32, (CHUNK, CHUNK), 1)
    causal = col <= row
    nt = (((1,), (1,)), ((), ()))
    tn = (((0,), (0,)), ((), ()))
    n_steps = NCH * GLA_HEADS
    step = 0
    for c in range(NCH):
        r0 = c * CHUNK
        for hd in range(GLA_HEADS):
            k0 = hd * GLA_HK
            v0 = hd * GLA_HV
            q_c = qt_ref[r0:r0 + CHUNK, k0:k0 + GLA_HK]
            k_c = kt_ref[r0:r0 + CHUNK, k0:k0 + GLA_HK]
            ke_c = kend_ref[r0:r0 + CHUNK, k0:k0 + GLA_HK]
            v_c = v_ref[r0:r0 + CHUNK, v0:v0 + GLA_HV]
            st = state_ref[hd]
            scores = lax.dot_general(q_c, k_c, nt, preferred_element_type=F32)
            scores = jnp.where(causal, scores, 0.0).astype(BF16)
            lhs = jnp.concatenate([q_c, scores], axis=1)
            rhs = jnp.concatenate([st.astype(BF16), v_c], axis=0)
            og_ref[r0:r0 + CHUNK, v0:v0 + GLA_HV] = jnp.dot(lhs, rhs, preferred_element_type=F32)
            kv = lax.dot_general(ke_c, v_c, tn, preferred_element_type=F32)
            state_ref[hd] = st * dect_ref[k0:k0 + GLA_HK, c:c + 1] + kv
            step += 1
            while pieces and len(pieces) * n_steps > (n_steps - step) * N_PIECES:
                pieces.pop(0)()
    while pieces:
        pieces.pop(0)()

    o_parts = []
    for hd in range(GLA_HEADS):
        v0 = hd * GLA_HV
        o_h = og_ref[:, v0:v0 + GLA_HV]
        mu = jnp.mean(o_h, axis=-1, keepdims=True)
        dlt = o_h - mu
        var = jnp.mean(dlt * dlt, axis=-1, keepdims=True)
        o_parts.append((dlt * lax.rsqrt(var + LN_EPS) * gng_ref[0, :, v0:v0 + GLA_HV]
                        * rg_ref[:, v0:v0 + GLA_HV]).astype(BF16))
        gg_piece(hd * PIECE)
    o_n = jnp.concatenate(o_parts, axis=-1)
    y_gla = jnp.dot(o_n, wpg_ref[0], preferred_element_type=F32)

    pos = s * TS + lax.broadcasted_iota(jnp.int32, (TS, POOL_GW), 0)
    p_parts = []
    for gi, w in enumerate(POOL_WINDOWS):
        c0 = gi * POOL_GW
        u_g = uext_ref[POOL_HALO:POOL_HALO + TS, c0:c0 + POOL_GW]
        wsum = u_g
        for sft in range(1, w):
            wsum = wsum + uext_ref[POOL_HALO - sft:POOL_HALO - sft + TS, c0:c0 + POOL_GW]
        cnt = jnp.minimum(pos + 1, w).astype(F32)
        p_g = (wsum / cnt - u_g).astype(BF16)
        p_parts.append(jnp.dot(p_g, wpool_ref[0, gi], preferred_element_type=F32))
    p = (jnp.concatenate(p_parts, axis=-1) * pscale_ref[0]).astype(BF16)
    y_pool = jnp.dot(p, wpp_ref[0], preferred_element_type=F32)

    merged = gg_ref[...] * y_gla + gp_ref[...] * y_pool
    y = jnp.dot(merged.astype(BF16), wout_ref[0], preferred_element_type=F32)
    t = ALPHA * x + (1.0 + gt) * y
    o_ref[0] = _layer_norm(t, lng_ref[0, 1:2, :], lnb_ref[0, 1:2, :])


def _mixer_call(x, mod, wts, ln_g, ln_b, *, layer):
    B, S, D = x.shape
    TS = MIX_TS
    operands = list(wts) + [ln_g, ln_b]
    return pl.pallas_call(
        _mixer_kernel,
        grid=(B, S // TS),
        in_specs=[
            pl.BlockSpec((1, TS, D), lambda b, s: (b, s, 0)),
            pl.BlockSpec((1, 1, N_MOD, D), lambda b, s: (layer, b, 0, 0)),
        ] + [_layer_spec(a, layer) for a in operands],
        out_specs=pl.BlockSpec((1, TS, D), lambda b, s: (b, s, 0)),
        out_shape=jax.ShapeDtypeStruct((B, S, D), F32),
        scratch_shapes=[
            pltpu.VMEM((GLA_HEADS, GLA_HK, GLA_HV), F32),
            pltpu.VMEM((POOL_HALO + TS, POOL_WIDTH), F32),
            pltpu.VMEM((TS, GLA_DK), BF16),
            pltpu.VMEM((TS, GLA_DK), BF16),
            pltpu.VMEM((TS, GLA_DK), BF16),
            pltpu.VMEM((TS, GLA_DV), BF16),
            pltpu.VMEM((GLA_DK, LANES), F32),
            pltpu.VMEM((TS, GLA_DV), F32),
            pltpu.VMEM((TS, GLA_DV), F32),
            pltpu.VMEM((TS, D_MODEL), F32),
            pltpu.VMEM((TS, D_MODEL), F32),
        ],
        compiler_params=pltpu.CompilerParams(
            dimension_semantics=("arbitrary", "arbitrary"),
            vmem_limit_bytes=VMEM_LIMIT),
        name="mixer",
    )(x, mod, *operands)


def _mixer_weights(w_in, w_alpha, b_alpha, gla_norm_g, w_proj_gla, w_pool, pool_scale,
                   w_proj_pool, w_out):
    L = w_in.shape[0]
    a0 = 2 * GLA_DK + 2 * GLA_DV
    a1 = a0 + GLA_RANK
    w_qkvr = w_in[:, :, :a0].astype(BF16)
    w_ugg = w_in[:, :, a1:].astype(BF16)
    w_a = jnp.pad(w_in[:, :, a0:a1], ((0, 0), (0, 0), (0, LANES - GLA_RANK))).astype(BF16)
    w_al = jnp.pad(w_alpha, ((0, 0), (0, LANES - GLA_RANK), (0, 0))).astype(BF16)
    return (
        w_qkvr, w_ugg, w_a, w_al, b_alpha.reshape(L, 1, GLA_DK), gla_norm_g.reshape(L, 1, GLA_DV),
        w_proj_gla.astype(BF16), w_pool.astype(BF16), pool_scale.reshape(L, 1, POOL_WIDTH),
        w_proj_pool.astype(BF16), w_out.astype(BF16),
    )


def kernel(x, c, w_ada, b_ada, ffn1_up, ffn1_down, w_in, w_alpha, b_alpha, gla_norm_g,
           w_proj_gla, w_pool, pool_scale, w_proj_pool, w_out, ffn2_up, ffn2_down, ln_g, ln_b):
    B, S, D = x.shape
    L = w_ada.shape[0]
    c_pad = jnp.pad(c, ((0, 8 - B), (0, 0)))
    mod = _ada_call(c_pad, w_ada, b_ada)[:, :B].reshape(L, B, N_MOD, D)
    wts = _mixer_weights(w_in, w_alpha, b_alpha, gla_norm_g, w_proj_gla, w_pool, pool_scale,
                         w_proj_pool, w_out)
    up1, dn1 = ffn1_up.astype(BF16), ffn1_down.astype(BF16)
    up2, dn2 = ffn2_up.astype(BF16), ffn2_down.astype(BF16)
    x2 = x.reshape(B * S, D)
    for l in range(L):
        x2 = _ffn_call(x2, mod, up1, dn1, ln_g, ln_b, layer=l, mod_base=0, ln_row=0, seq=S)
        x3 = _mixer_call(x2.reshape(B, S, D), mod, wts, ln_g, ln_b, layer=l)
        x2 = _ffn_call(x3.reshape(B * S, D), mod, up2, dn2, ln_g, ln_b, layer=l, mod_base=6,
                       ln_row=2, seq=S)
    return x2.reshape(B, S, D)
```

```python
import functools

import jax
import jax.numpy as jnp
from jax import lax
from jax.experimental import pallas as pl
from jax.experimental.pallas import tpu as pltpu

F32 = jnp.float32
BF16 = jnp.bfloat16

D_MODEL = 1024
DEPTH = 4
GLA_HEADS = 4
GLA_DK = 512
GLA_DV = 1024
GLA_HK = 128
GLA_HV = 256
GLA_RANK = 16
GLA_TAU = 16.0
CHUNK = 64
POOL_WIDTH = 512
POOL_WINDOWS = (2, 4, 8, 16)
POOL_GW = 128
POOL_HALO = 16
D_FF = 2816
N_MOD = 9
ALPHA = (2 * DEPTH) ** 0.25
LN_EPS = 1e-5
LANES = 128

FFN_TM = 512
FFN_TF = 256
MIX_TS = 512
LN_ROWS = 64
STAGE_COLS = 512
STAGE_ROWS = 256
CUM_ROWS = 256
ADA_TN = 2304
VMEM_LIMIT = 56 * 1024 * 1024

QKVR_Q = 0
QKVR_K = QKVR_Q + GLA_DK
QKVR_V = QKVR_K + GLA_DK
QKVR_R = QKVR_V + GLA_DV
UGG_U = 0
UGG_GG = UGG_U + POOL_WIDTH
UGG_GP = UGG_GG + D_MODEL
W_QKVR = QKVR_R + GLA_DV
W_UGG = UGG_GP + D_MODEL
PIECE = 256
N_PIECES = (GLA_DV + POOL_WIDTH + D_MODEL) // PIECE
CUM_TERMS = 2


def _layer_norm(t, g, b):
    mu = jnp.mean(t, axis=-1, keepdims=True)
    d = t - mu
    var = jnp.mean(d * d, axis=-1, keepdims=True)
    return d * lax.rsqrt(var + LN_EPS) * g + b


def _silu(t):
    return t * jax.nn.sigmoid(t)


def _zero_bits_of(v):
    bits = lax.bitcast_convert_type(v, jnp.uint32)
    acc = None
    for r0 in range(0, v.shape[0], 8):
        for c0 in range(0, v.shape[1], LANES):
            tile = bits[r0:r0 + 8, c0:c0 + LANES]
            acc = tile if acc is None else acc | tile
    return lax.shift_right_logical(lax.shift_right_logical(acc, jnp.uint32(16)), jnp.uint32(16))


def _after(v, zero_bits):
    reps = v.shape[0] // zero_bits.shape[0]
    z = jnp.concatenate([zero_bits] * reps, axis=0) if reps > 1 else zero_bits
    return jnp.where(z == 0, v, 0.0)


def _stage_bf16(windows, stage_ref, sems):
    def copy(j):
        src, _ = windows[j]
        slot = stage_ref.at[j % 2, 0:src.shape[0], 0:src.shape[1]]
        return pltpu.make_async_copy(src, slot, sems.at[j % 2])

    copy(0).start()
    for j, (src, dst) in enumerate(windows):
        if j + 1 < len(windows):
            copy(j + 1).start()
        copy(j).wait()
        dst[...] = stage_ref[j % 2, 0:src.shape[0], 0:src.shape[1]].astype(BF16)


def _col_windows(src, dst, src_col0, width):
    return [(src.at[:, pl.ds(src_col0 + c0, STAGE_COLS)], dst.at[:, pl.ds(c0, STAGE_COLS)])
            for c0 in range(0, width, STAGE_COLS)]


def _layer_spec(a, layer):
    nd = a.ndim
    return pl.BlockSpec((1,) + a.shape[1:], lambda *_: (layer,) + (0,) * (nd - 1),
                        pipeline_mode=pl.Buffered(1))


def _ada_kernel(c_ref, w_ref, b_ref, o_ref):
    c_act = _silu(c_ref[...]).astype(BF16)
    w = w_ref[0].astype(BF16)
    o_ref[0] = jnp.dot(c_act, w, preferred_element_type=F32) + b_ref[0]


def _ada_call(c_pad, w_ada, b_ada):
    L, D, N = w_ada.shape
    rows = c_pad.shape[0]
    return pl.pallas_call(
        _ada_kernel,
        grid=(L, N // ADA_TN),
        in_specs=[
            pl.BlockSpec((rows, D), lambda l, n: (0, 0)),
            pl.BlockSpec((1, D, ADA_TN), lambda l, n: (l, 0, n)),
            pl.BlockSpec((1, 1, ADA_TN), lambda l, n: (l, 0, n)),
        ],
        out_specs=pl.BlockSpec((1, rows, ADA_TN), lambda l, n: (l, 0, n)),
        out_shape=jax.ShapeDtypeStruct((L, rows, N), F32),
        compiler_params=pltpu.CompilerParams(
            dimension_semantics=("arbitrary", "arbitrary"),
            vmem_limit_bytes=VMEM_LIMIT),
        name="ada_mod",
    )(c_pad, w_ada, b_ada.reshape(L, 1, N))


def _ffn_kernel(xp_ref, x_ref, modp_ref, mod_ref, wup_hbm, wdn_hbm, lng_ref, lnb_ref,
                o_ref, a_ref, y_ref, wup_ref, wdn_ref, upstage_ref, dnstage_ref, sems,
                *, layer, mod_base, ln_row, n_tiles):
    i = pl.program_id(0)

    def post_norm(rb):
        rows = pl.ds(rb * LN_ROWS, LN_ROWS)
        gt = modp_ref[0, 0, mod_base + 2:mod_base + 3, :]
        t = ALPHA * xp_ref[rows, :] + (0.5 * (1.0 + gt)) * y_ref[rows, :]
        out = _layer_norm(t, lng_ref[0, ln_row:ln_row + 1, :], lnb_ref[0, ln_row:ln_row + 1, :])
        o_ref[rows, :] = out
        return out

    @pl.when(i == 0)
    def _():
        y_ref[...] = jnp.zeros_like(y_ref)
        _stage_bf16(_col_windows(wup_hbm.at[layer], wup_ref, 0, 2 * D_FF), upstage_ref, sems.at[0])
        _stage_bf16([(wdn_hbm.at[layer, pl.ds(r0, STAGE_ROWS), :], wdn_ref.at[pl.ds(r0, STAGE_ROWS), :])
                     for r0 in range(0, D_FF, STAGE_ROWS)], dnstage_ref, sems.at[1])

    @pl.when(i < n_tiles)
    def _():
        sh = mod_ref[0, 0, mod_base:mod_base + 1, :]
        sc = mod_ref[0, 0, mod_base + 1:mod_base + 2, :]
        h = (x_ref[...] * (1.0 + sc) + sh).astype(BF16)
        anchor = None
        for ci, c0 in enumerate(range(0, D_FF, FFN_TF)):
            g = jnp.dot(h, wup_ref[:, c0:c0 + FFN_TF], preferred_element_type=F32)
            u = jnp.dot(h, wup_ref[:, D_FF + c0:D_FF + c0 + FFN_TF], preferred_element_type=F32)
            act = _silu(g) * u
            a_ref[:, c0:c0 + FFN_TF] = act.astype(BF16)
            if anchor is not None:
                a_ref[0:16, c0:c0 + LANES] = _after(act[0:16, 0:LANES], anchor).astype(BF16)
            anchor = _zero_bits_of(post_norm(ci)) if ci < FFN_TM // LN_ROWS else None
        y_ref[...] = jnp.dot(a_ref[...], wdn_ref[...], preferred_element_type=F32)

    @pl.when(i == n_tiles)
    def _():
        for rb in range(FFN_TM // LN_ROWS):
            post_norm(rb)


def _ffn_call(x2, mod, w_up, w_down, ln_g, ln_b, *, layer, mod_base, ln_row, seq):
    M, D = x2.shape
    n_tiles = M // FFN_TM
    tiles_per_seq = seq // FFN_TM
    assert D_FF // FFN_TF >= FFN_TM // LN_ROWS

    def cur(i):
        return jnp.minimum(i, n_tiles - 1)

    def prev(i):
        return jnp.maximum(i - 1, 0)

    return pl.pallas_call(
        functools.partial(_ffn_kernel, layer=layer, mod_base=mod_base, ln_row=ln_row,
                          n_tiles=n_tiles),
        grid=(n_tiles + 1,),
        in_specs=[
            pl.BlockSpec((FFN_TM, D), lambda i: (prev(i), 0)),
            pl.BlockSpec((FFN_TM, D), lambda i: (cur(i), 0)),
            pl.BlockSpec((1, 1, N_MOD, D), lambda i: (layer, prev(i) // tiles_per_seq, 0, 0)),
            pl.BlockSpec((1, 1, N_MOD, D), lambda i: (layer, cur(i) // tiles_per_seq, 0, 0)),
            pl.BlockSpec(memory_space=pl.ANY),
            pl.BlockSpec(memory_space=pl.ANY),
            _layer_spec(ln_g, layer),
            _layer_spec(ln_b, layer),
        ],
        out_specs=pl.BlockSpec((FFN_TM, D), lambda i: (prev(i), 0)),
        out_shape=jax.ShapeDtypeStruct((M, D), F32),
        scratch_shapes=[
            pltpu.VMEM((FFN_TM, D_FF), BF16),
            pltpu.VMEM((FFN_TM, D), F32),
            pltpu.VMEM((D, 2 * D_FF), BF16),
            pltpu.VMEM((D_FF, D), BF16),
            pltpu.VMEM((2, D, STAGE_COLS), F32),
            pltpu.VMEM((2, STAGE_ROWS, D), F32),
            pltpu.SemaphoreType.DMA((2, 2)),
        ],
        compiler_params=pltpu.CompilerParams(
            dimension_semantics=("arbitrary",),
            vmem_limit_bytes=VMEM_LIMIT),
        name="ffn",
    )(x2, x2, mod, mod, w_up, w_down, ln_g, ln_b)


def _split_bf16(t, parts):
    out = []
    r = t
    for _ in range(parts):
        p = r.astype(BF16)
        out.append(p)
        r = r - p.astype(F32)
    return out


def _mixer_kernel(x_ref, mod_ref, win_hbm, wpg_hbm, wpp_hbm, wout_hbm, wugg_slab_ref, walpha_ref,
                  balpha_ref, gng_ref, wpool_ref, pscale_ref, lng_ref, lnb_ref,
                  o_ref,
                  state_ref, uext_ref, qt_ref, kt_ref, kend_ref, v_ref, dect_ref, og_ref,
                  rg_ref, gg_ref, gp_ref,
                  wqkvr_ref, wa_ref, wpg_ref, wpp_ref, wout_ref, stage_ref, sems,
                  *, layer):
    s = pl.program_id(1)
    TS = MIX_TS
    NCH = TS // CHUNK
    wugg_ref = wugg_slab_ref.at[0]

    @pl.when((pl.program_id(0) == 0) & (s == 0))
    def _():
        w_in = win_hbm.at[layer]
        _stage_bf16(
            _col_windows(w_in, wqkvr_ref, 0, W_QKVR)
            + [(w_in.at[:, pl.ds(W_QKVR, LANES)], wa_ref)]
            + _col_windows(wpg_hbm.at[layer], wpg_ref, 0, D_MODEL)
            + _col_windows(wout_hbm.at[layer], wout_ref, 0, D_MODEL)
            + _col_windows(wpp_hbm.at[layer], wpp_ref, 0, D_MODEL),
            stage_ref, sems)
        wa_ref[:, GLA_RANK:LANES] = jnp.zeros((D_MODEL, LANES - GLA_RANK), BF16)

    @pl.when(s == 0)
    def _():
        state_ref[...] = jnp.zeros_like(state_ref)
        uext_ref[0:POOL_HALO, :] = jnp.zeros((POOL_HALO, POOL_WIDTH), F32)

    @pl.when(s > 0)
    def _():
        uext_ref[0:POOL_HALO, :] = uext_ref[TS:TS + POOL_HALO, :]

    x = x_ref[0]
    sh = mod_ref[0, 0, 3:4, :]
    sc = mod_ref[0, 0, 4:5, :]
    gt = mod_ref[0, 0, 5:6, :]
    h = (x * (1.0 + sc) + sh).astype(BF16)

    def proj(w_ref, c0, c1):
        return jnp.dot(h, w_ref[:, c0:c1], preferred_element_type=F32)

    a_lr = jnp.dot(h, wa_ref[...], preferred_element_type=F32).astype(BF16)
    pre = jnp.dot(a_lr, walpha_ref[0], preferred_element_type=F32) + balpha_ref[0]
    log_a = (jnp.minimum(pre, 0.0) - jnp.log1p(jnp.exp(-jnp.abs(pre)))) / GLA_TAU

    v_ref[...] = proj(wqkvr_ref, QKVR_V, QKVR_R).astype(BF16)

    ri = lax.broadcasted_iota(jnp.int32, (CUM_ROWS, CUM_ROWS), 0)
    ci = lax.broadcasted_iota(jnp.int32, (CUM_ROWS, CUM_ROWS), 1)
    tri = jnp.where((ri // CHUNK == ci // CHUNK) & (ci <= ri), 1.0, 0.0).astype(BF16)
    b_parts = []
    for g0 in range(0, TS, CUM_ROWS):
        acc = None
        for part in _split_bf16(log_a[g0:g0 + CUM_ROWS], CUM_TERMS):
            t = jnp.dot(tri, part, preferred_element_type=F32)
            acc = t if acc is None else acc + t
        b_parts.append(acc)
    b = jnp.concatenate(b_parts, axis=0)

    q = proj(wqkvr_ref, QKVR_Q, QKVR_K) * (GLA_HK ** -0.5)
    k = proj(wqkvr_ref, QKVR_K, QKVR_V)
    qt_ref[...] = (q * jnp.exp(b)).astype(BF16)
    kt_ref[...] = (k * jnp.exp(-b)).astype(BF16)
    dec_rows = []
    for c in range(NCH):
        r0 = c * CHUNK
        b_c = b[r0:r0 + CHUNK]
        b_last = b_c[CHUNK - 1:CHUNK, :]
        kend_ref[r0:r0 + CHUNK, :] = (k[r0:r0 + CHUNK] * jnp.exp(b_last - b_c)).astype(BF16)
        dec_rows.append(jnp.exp(b_last))
    dec_all = jnp.concatenate(
        dec_rows + [jnp.zeros((LANES - NCH, GLA_DK), F32)], axis=0)
    dect_ref[...] = dec_all.T

    def r_piece(c0):
        rg_ref[:, c0:c0 + PIECE] = _silu(proj(wqkvr_ref, QKVR_R + c0, QKVR_R + c0 + PIECE))

    def u_piece(c0):
        uext_ref[POOL_HALO:POOL_HALO + TS, c0:c0 + PIECE] = proj(wugg_ref, UGG_U + c0, UGG_U + c0 + PIECE)

    def gp_piece(c0):
        gp_ref[:, c0:c0 + PIECE] = jax.nn.sigmoid(proj(wugg_ref, UGG_GP + c0, UGG_GP + c0 + PIECE))

    def gg_piece(c0):
        gg_ref[:, c0:c0 + PIECE] = jax.nn.sigmoid(proj(wugg_ref, UGG_GG + c0, UGG_GG + c0 + PIECE))

    pieces = ([functools.partial(r_piece, c0) for c0 in range(0, GLA_DV, PIECE)]
              + [functools.partial(u_piece, c0) for c0 in range(0, POOL_WIDTH, PIECE)]
              + [functools.partial(gp_piece, c0) for c0 in range(0, D_MODEL, PIECE)])

    row = lax.broadcasted_iota(jnp.int32, (CHUNK, CHUNK), 0)
    col = lax.broadcasted_iota(jnp.int32, (CHUNK, CHUNK), 1)
    causal = col <= row
    nt = (((1,), (1,)), ((), ()))
    tn = (((0,), (0,)), ((), ()))
    n_steps = NCH * GLA_HEADS
    step = 0
    for c in range(NCH):
        r0 = c * CHUNK
        for hd in range(GLA_HEADS):
            k0 = hd * GLA_HK
            v0 = hd * GLA_HV
            q_c = qt_ref[r0:r0 + CHUNK, k0:k0 + GLA_HK]
            k_c = kt_ref[r0:r0 + CHUNK, k0:k0 + GLA_HK]
            ke_c = kend_ref[r0:r0 + CHUNK, k0:k0 + GLA_HK]
            v_c = v_ref[r0:r0 + CHUNK, v0:v0 + GLA_HV]
            st = state_ref[hd]
            scores = lax.dot_general(q_c, k_c, nt, preferred_element_type=F32)
            scores = jnp.where(causal, scores, 0.0).astype(BF16)
            lhs = jnp.concatenate([q_c, scores], axis=1)
            rhs = jnp.concatenate([st.astype(BF16), v_c], axis=0)
            og_ref[r0:r0 + CHUNK, v0:v0 + GLA_HV] = jnp.dot(lhs, rhs, preferred_element_type=F32)
            kv = lax.dot_general(ke_c, v_c, tn, preferred_element_type=F32)
            state_ref[hd] = st * dect_ref[k0:k0 + GLA_HK, c:c + 1] + kv
            step += 1
            while pieces and len(pieces) * n_steps > (n_steps - step) * N_PIECES:
                pieces.pop(0)()
    while pieces:
        pieces.pop(0)()

    o_parts = []
    for hd in range(GLA_HEADS):
        v0 = hd * GLA_HV
        o_h = og_ref[:, v0:v0 + GLA_HV]
        mu = jnp.mean(o_h, axis=-1, keepdims=True)
        dlt = o_h - mu
        var = jnp.mean(dlt * dlt, axis=-1, keepdims=True)
        o_parts.append((dlt * lax.rsqrt(var + LN_EPS) * gng_ref[0, :, v0:v0 + GLA_HV]
                        * rg_ref[:, v0:v0 + GLA_HV]).astype(BF16))
        gg_piece(hd * PIECE)
    o_n = jnp.concatenate(o_parts, axis=-1)
    y_gla = jnp.dot(o_n, wpg_ref[...], preferred_element_type=F32)

    pos = s * TS + lax.broadcasted_iota(jnp.int32, (TS, POOL_GW), 0)
    p_parts = []
    for gi, w in enumerate(POOL_WINDOWS):
        c0 = gi * POOL_GW
        u_g = uext_ref[POOL_HALO:POOL_HALO + TS, c0:c0 + POOL_GW]
        wsum = u_g
        for sft in range(1, w):
            wsum = wsum + uext_ref[POOL_HALO - sft:POOL_HALO - sft + TS, c0:c0 + POOL_GW]
        cnt = jnp.minimum(pos + 1, w).astype(F32)
        p_g = (wsum / cnt - u_g).astype(BF16)
        p_parts.append(jnp.dot(p_g, wpool_ref[0, gi], preferred_element_type=F32))
    p = (jnp.concatenate(p_parts, axis=-1) * pscale_ref[0]).astype(BF16)
    y_pool = jnp.dot(p, wpp_ref[...], preferred_element_type=F32)

    merged = gg_ref[...] * y_gla + gp_ref[...] * y_pool
    y = jnp.dot(merged.astype(BF16), wout_ref[...], preferred_element_type=F32)
    t = ALPHA * x + (1.0 + gt) * y
    o_ref[0] = _layer_norm(t, lng_ref[0, 1:2, :], lnb_ref[0, 1:2, :])


def _mixer_call(x, mod, big_weights, small_operands, *, layer):
    B, S, D = x.shape
    TS = MIX_TS
    return pl.pallas_call(
        functools.partial(_mixer_kernel, layer=layer),
        grid=(B, S // TS),
        in_specs=[
            pl.BlockSpec((1, TS, D), lambda b, s: (b, s, 0)),
            pl.BlockSpec((1, 1, N_MOD, D), lambda b, s: (layer, b, 0, 0)),
        ] + [pl.BlockSpec(memory_space=pl.ANY) for _ in big_weights]
          + [_layer_spec(a, layer) for a in small_operands],
        out_specs=pl.BlockSpec((1, TS, D), lambda b, s: (b, s, 0)),
        out_shape=jax.ShapeDtypeStruct((B, S, D), F32),
        scratch_shapes=[
            pltpu.VMEM((GLA_HEADS, GLA_HK, GLA_HV), F32),
            pltpu.VMEM((POOL_HALO + TS, POOL_WIDTH), F32),
            pltpu.VMEM((TS, GLA_DK), BF16),
            pltpu.VMEM((TS, GLA_DK), BF16),
            pltpu.VMEM((TS, GLA_DK), BF16),
            pltpu.VMEM((TS, GLA_DV), BF16),
            pltpu.VMEM((GLA_DK, LANES), F32),
            pltpu.VMEM((TS, GLA_DV), F32),
            pltpu.VMEM((TS, GLA_DV), F32),
            pltpu.VMEM((TS, D_MODEL), F32),
            pltpu.VMEM((TS, D_MODEL), F32),
            pltpu.VMEM((D_MODEL, W_QKVR), BF16),
            pltpu.VMEM((D_MODEL, LANES), BF16),
            pltpu.VMEM((GLA_DV, D_MODEL), BF16),
            pltpu.VMEM((POOL_WIDTH, D_MODEL), BF16),
            pltpu.VMEM((D_MODEL, D_MODEL), BF16),
            pltpu.VMEM((2, D_MODEL, STAGE_COLS), F32),
            pltpu.SemaphoreType.DMA((2,)),
        ],
        compiler_params=pltpu.CompilerParams(
            dimension_semantics=("arbitrary", "arbitrary"),
            vmem_limit_bytes=VMEM_LIMIT),
        name="mixer",
    )(x, mod, *big_weights, *small_operands)


def _mixer_slab_operands(w_in, w_alpha, b_alpha, gla_norm_g, w_pool, pool_scale, ln_g, ln_b):
    L = w_alpha.shape[0]
    w_ugg = w_in[:, :, W_QKVR + GLA_RANK:].astype(BF16)
    w_al = jnp.pad(w_alpha, ((0, 0), (0, LANES - GLA_RANK), (0, 0))).astype(BF16)
    return (w_ugg, w_al, b_alpha.reshape(L, 1, GLA_DK), gla_norm_g.reshape(L, 1, GLA_DV),
            w_pool.astype(BF16), pool_scale.reshape(L, 1, POOL_WIDTH), ln_g, ln_b)


def kernel(x, c, w_ada, b_ada, ffn1_up, ffn1_down, w_in, w_alpha, b_alpha, gla_norm_g,
           w_proj_gla, w_pool, pool_scale, w_proj_pool, w_out, ffn2_up, ffn2_down, ln_g, ln_b):
    B, S, D = x.shape
    L = w_ada.shape[0]
    c_pad = jnp.pad(c, ((0, 8 - B), (0, 0)))
    mod = _ada_call(c_pad, w_ada, b_ada)[:, :B].reshape(L, B, N_MOD, D)
    mix_big = (w_in, w_proj_gla, w_proj_pool, w_out)
    mix_small = _mixer_slab_operands(w_in, w_alpha, b_alpha, gla_norm_g, w_pool, pool_scale,
                                     ln_g, ln_b)
    x2 = x.reshape(B * S, D)
    for l in range(L):
        x2 = _ffn_call(x2, mod, ffn1_up, ffn1_down, ln_g, ln_b, layer=l, mod_base=0, ln_row=0,
                       seq=S)
        x3 = _mixer_call(x2.reshape(B, S, D), mod, mix_big, mix_small, layer=l)
        x2 = _ffn_call(x3.reshape(B * S, D), mod, ffn2_up, ffn2_down, ln_g, ln_b, layer=l,
                       mod_base=6, ln_row=2, seq=S)
    return x2.reshape(B, S, D)
```

```python
import functools

import jax
import jax.numpy as jnp
from jax import lax
from jax.experimental import pallas as pl
from jax.experimental.pallas import tpu as pltpu

F32 = jnp.float32
BF16 = jnp.bfloat16

D_MODEL = 1024
DEPTH = 4
GLA_HEADS = 4
GLA_DK = 512
GLA_DV = 1024
GLA_HK = 128
GLA_HV = 256
GLA_RANK = 16
GLA_TAU = 16.0
CHUNK = 64
POOL_WIDTH = 512
POOL_WINDOWS = (2, 4, 8, 16)
POOL_GW = 128
POOL_HALO = 16
D_FF = 2816
N_MOD = 9
ALPHA = (2 * DEPTH) ** 0.25
LN_EPS = 1e-5
LANES = 128

FFN_TM = 512
FFN_TF = 256
MIX_TS = 512
LN_ROWS = 64
STAGE_COLS = 512
STAGE_ROWS = 256
MIX_STAGE_ROWS = 512
CUM_ROWS = 256
ADA_TN = 2304
VMEM_LIMIT = 56 * 1024 * 1024

IN_Q = 0
IN_K = IN_Q + GLA_DK
IN_V = IN_K + GLA_DK
IN_R = IN_V + GLA_DV
IN_A = IN_R + GLA_DV
IN_U = IN_A + GLA_RANK
IN_GG = IN_U + POOL_WIDTH
IN_GP = IN_GG + D_MODEL
IN_WIDTH = IN_GP + D_MODEL
PIECE = 256
N_PIECES = (GLA_DV + POOL_WIDTH + D_MODEL) // PIECE
CUM_TERMS = 2


def _layer_norm(t, g, b):
    mu = jnp.mean(t, axis=-1, keepdims=True)
    d = t - mu
    var = jnp.mean(d * d, axis=-1, keepdims=True)
    return d * lax.rsqrt(var + LN_EPS) * g + b


def _silu(t):
    return t * jax.nn.sigmoid(t)


def _zero_bits_of(v):
    bits = lax.bitcast_convert_type(v, jnp.uint32)
    acc = None
    for r0 in range(0, v.shape[0], 8):
        for c0 in range(0, v.shape[1], LANES):
            tile = bits[r0:r0 + 8, c0:c0 + LANES]
            acc = tile if acc is None else acc | tile
    return lax.shift_right_logical(lax.shift_right_logical(acc, jnp.uint32(16)), jnp.uint32(16))


def _after(v, zero_bits):
    reps = v.shape[0] // zero_bits.shape[0]
    z = jnp.concatenate([zero_bits] * reps, axis=0) if reps > 1 else zero_bits
    return jnp.where(z == 0, v, 0.0)


def _stage_bf16(windows, stage_ref, sems):
    def copy(j):
        src, _ = windows[j]
        slot = stage_ref.at[j % 2, 0:src.shape[0], 0:src.shape[1]]
        return pltpu.make_async_copy(src, slot, sems.at[j % 2])

    copy(0).start()
    for j, (src, dst) in enumerate(windows):
        if j + 1 < len(windows):
            copy(j + 1).start()
        copy(j).wait()
        dst[...] = stage_ref[j % 2, 0:src.shape[0], 0:src.shape[1]].astype(BF16)


def _col_windows(src, dst, width):
    return [(src.at[:, pl.ds(c0, STAGE_COLS)], dst.at[:, pl.ds(c0, STAGE_COLS)])
            for c0 in range(0, width, STAGE_COLS)]


def _row_windows(src, dst, n_rows, step):
    return [(src.at[pl.ds(r0, min(step, n_rows - r0)), :], dst.at[pl.ds(r0, min(step, n_rows - r0)), :])
            for r0 in range(0, n_rows, step)]


def _layer_spec(a, layer):
    nd = a.ndim
    return pl.BlockSpec((1,) + a.shape[1:], lambda *_: (layer,) + (0,) * (nd - 1),
                        pipeline_mode=pl.Buffered(1))


def _ada_kernel(c_ref, w_ref, b_ref, o_ref):
    c_act = _silu(c_ref[...]).astype(BF16)
    w = w_ref[0].astype(BF16)
    o_ref[0] = jnp.dot(c_act, w, preferred_element_type=F32) + b_ref[0]


def _ada_call(c_pad, w_ada, b_ada):
    L, D, N = w_ada.shape
    rows = c_pad.shape[0]
    return pl.pallas_call(
        _ada_kernel,
        grid=(L, N // ADA_TN),
        in_specs=[
            pl.BlockSpec((rows, D), lambda l, n: (0, 0)),
            pl.BlockSpec((1, D, ADA_TN), lambda l, n: (l, 0, n)),
            pl.BlockSpec((1, 1, ADA_TN), lambda l, n: (l, 0, n)),
        ],
        out_specs=pl.BlockSpec((1, rows, ADA_TN), lambda l, n: (l, 0, n)),
        out_shape=jax.ShapeDtypeStruct((L, rows, N), F32),
        compiler_params=pltpu.CompilerParams(
            dimension_semantics=("arbitrary", "arbitrary"),
            vmem_limit_bytes=VMEM_LIMIT),
        name="ada_mod",
    )(c_pad, w_ada, b_ada.reshape(L, 1, N))


def _ffn_kernel(xp_ref, x_ref, modp_ref, mod_ref, wup_hbm, wdn_hbm, lng_ref, lnb_ref,
                o_ref, a_ref, y_ref, wup_ref, wdn_ref, upstage_ref, dnstage_ref, up_sems, dn_sems,
                *, layer, mod_base, ln_row, n_tiles):
    i = pl.program_id(0)

    def post_norm(rb):
        rows = pl.ds(rb * LN_ROWS, LN_ROWS)
        gt = modp_ref[0, 0, mod_base + 2:mod_base + 3, :]
        t = ALPHA * xp_ref[rows, :] + (0.5 * (1.0 + gt)) * y_ref[rows, :]
        out = _layer_norm(t, lng_ref[0, ln_row:ln_row + 1, :], lnb_ref[0, ln_row:ln_row + 1, :])
        o_ref[rows, :] = out
        return out

    @pl.when(i == 0)
    def _():
        y_ref[...] = jnp.zeros_like(y_ref)
        _stage_bf16(_col_windows(wup_hbm.at[layer], wup_ref, 2 * D_FF), upstage_ref, up_sems)
        _stage_bf16(_row_windows(wdn_hbm.at[layer], wdn_ref, D_FF, STAGE_ROWS), dnstage_ref, dn_sems)

    @pl.when(i < n_tiles)
    def _():
        sh = mod_ref[0, 0, mod_base:mod_base + 1, :]
        sc = mod_ref[0, 0, mod_base + 1:mod_base + 2, :]
        h = (x_ref[...] * (1.0 + sc) + sh).astype(BF16)
        anchor = None
        for ci, c0 in enumerate(range(0, D_FF, FFN_TF)):
            g = jnp.dot(h, wup_ref[:, c0:c0 + FFN_TF], preferred_element_type=F32)
            u = jnp.dot(h, wup_ref[:, D_FF + c0:D_FF + c0 + FFN_TF], preferred_element_type=F32)
            act = _silu(g) * u
            a_ref[:, c0:c0 + FFN_TF] = act.astype(BF16)
            if anchor is not None:
                a_ref[0:16, c0:c0 + LANES] = _after(act[0:16, 0:LANES], anchor).astype(BF16)
            anchor = _zero_bits_of(post_norm(ci)) if ci < FFN_TM // LN_ROWS else None
        y_ref[...] = jnp.dot(a_ref[...], wdn_ref[...], preferred_element_type=F32)

    @pl.when(i == n_tiles)
    def _():
        for rb in range(FFN_TM // LN_ROWS):
            post_norm(rb)


def _ffn_call(x2, mod, w_up, w_down, ln_g, ln_b, *, layer, mod_base, ln_row, seq):
    M, D = x2.shape
    n_tiles = M // FFN_TM
    tiles_per_seq = seq // FFN_TM
    assert D_FF // FFN_TF >= FFN_TM // LN_ROWS

    def cur(i):
        return jnp.minimum(i, n_tiles - 1)

    def prev(i):
        return jnp.maximum(i - 1, 0)

    return pl.pallas_call(
        functools.partial(_ffn_kernel, layer=layer, mod_base=mod_base, ln_row=ln_row,
                          n_tiles=n_tiles),
        grid=(n_tiles + 1,),
        in_specs=[
            pl.BlockSpec((FFN_TM, D), lambda i: (prev(i), 0)),
            pl.BlockSpec((FFN_TM, D), lambda i: (cur(i), 0)),
            pl.BlockSpec((1, 1, N_MOD, D), lambda i: (layer, prev(i) // tiles_per_seq, 0, 0)),
            pl.BlockSpec((1, 1, N_MOD, D), lambda i: (layer, cur(i) // tiles_per_seq, 0, 0)),
            pl.BlockSpec(memory_space=pl.ANY),
            pl.BlockSpec(memory_space=pl.ANY),
            _layer_spec(ln_g, layer),
            _layer_spec(ln_b, layer),
        ],
        out_specs=pl.BlockSpec((FFN_TM, D), lambda i: (prev(i), 0)),
        out_shape=jax.ShapeDtypeStruct((M, D), F32),
        scratch_shapes=[
            pltpu.VMEM((FFN_TM, D_FF), BF16),
            pltpu.VMEM((FFN_TM, D), F32),
            pltpu.VMEM((D, 2 * D_FF), BF16),
            pltpu.VMEM((D_FF, D), BF16),
            pltpu.VMEM((2, D, STAGE_COLS), F32),
            pltpu.VMEM((2, STAGE_ROWS, D), F32),
            pltpu.SemaphoreType.DMA((2,)),
            pltpu.SemaphoreType.DMA((2,)),
        ],
        compiler_params=pltpu.CompilerParams(
            dimension_semantics=("arbitrary",),
            vmem_limit_bytes=VMEM_LIMIT),
        name="ffn",
    )(x2, x2, mod, mod, w_up, w_down, ln_g, ln_b)


def _split_bf16(t, parts):
    out = []
    r = t
    for _ in range(parts):
        p = r.astype(BF16)
        out.append(p)
        r = r - p.astype(F32)
    return out


def _mixer_kernel(x_ref, mod_ref, wint_hbm, wpg_hbm, wpp_hbm, wout_hbm, walpha_ref,
                  balpha_ref, gng_ref, wpool_ref, pscale_ref, lng_ref, lnb_ref,
                  o_ref,
                  state_ref, uext_ref, qt_ref, kt_ref, kend_ref, v_ref, dect_ref, og_ref,
                  rg_ref, gg_ref, gp_ref,
                  wint_ref, wpg_ref, wpp_ref, wout_ref, stage_ref, sems,
                  *, layer):
    s = pl.program_id(1)
    TS = MIX_TS
    NCH = TS // CHUNK

    @pl.when((pl.program_id(0) == 0) & (s == 0))
    def _():
        _stage_bf16(
            _row_windows(wint_hbm.at[layer], wint_ref, IN_WIDTH, MIX_STAGE_ROWS)
            + _row_windows(wpg_hbm.at[layer], wpg_ref, GLA_DV, MIX_STAGE_ROWS)
            + _row_windows(wout_hbm.at[layer], wout_ref, D_MODEL, MIX_STAGE_ROWS)
            + _row_windows(wpp_hbm.at[layer], wpp_ref, POOL_WIDTH, MIX_STAGE_ROWS),
            stage_ref, sems)

    @pl.when(s == 0)
    def _():
        state_ref[...] = jnp.zeros_like(state_ref)
        uext_ref[0:POOL_HALO, :] = jnp.zeros((POOL_HALO, POOL_WIDTH), F32)

    @pl.when(s > 0)
    def _():
        uext_ref[0:POOL_HALO, :] = uext_ref[TS:TS + POOL_HALO, :]

    x = x_ref[0]
    sh = mod_ref[0, 0, 3:4, :]
    sc = mod_ref[0, 0, 4:5, :]
    gt = mod_ref[0, 0, 5:6, :]
    h = (x * (1.0 + sc) + sh).astype(BF16)

    nt = (((1,), (1,)), ((), ()))

    def proj(r0, r1):
        return lax.dot_general(h, wint_ref[r0:r1, :], nt, preferred_element_type=F32)

    a_wide = proj(IN_A, IN_A + LANES)
    lane = lax.broadcasted_iota(jnp.int32, a_wide.shape, 1)
    a_lr = jnp.where(lane < GLA_RANK, a_wide, 0.0).astype(BF16)
    pre = jnp.dot(a_lr, walpha_ref[0], preferred_element_type=F32) + balpha_ref[0]
    log_a = (jnp.minimum(pre, 0.0) - jnp.log1p(jnp.exp(-jnp.abs(pre)))) / GLA_TAU

    v_ref[...] = proj(IN_V, IN_R).astype(BF16)

    ri = lax.broadcasted_iota(jnp.int32, (CUM_ROWS, CUM_ROWS), 0)
    ci = lax.broadcasted_iota(jnp.int32, (CUM_ROWS, CUM_ROWS), 1)
    tri = jnp.where((ri // CHUNK == ci // CHUNK) & (ci <= ri), 1.0, 0.0).astype(BF16)
    b_parts = []
    for g0 in range(0, TS, CUM_ROWS):
        acc = None
        for part in _split_bf16(log_a[g0:g0 + CUM_ROWS], CUM_TERMS):
            t = jnp.dot(tri, part, preferred_element_type=F32)
            acc = t if acc is None else acc + t
        b_parts.append(acc)
    b = jnp.concatenate(b_parts, axis=0)

    q = proj(IN_Q, IN_K) * (GLA_HK ** -0.5)
    k = proj(IN_K, IN_V)
    qt_ref[...] = (q * jnp.exp(b)).astype(BF16)
    kt_ref[...] = (k * jnp.exp(-b)).astype(BF16)
    dec_rows = []
    for c in range(NCH):
        r0 = c * CHUNK
        b_c = b[r0:r0 + CHUNK]
        b_last = b_c[CHUNK - 1:CHUNK, :]
        kend_ref[r0:r0 + CHUNK, :] = (k[r0:r0 + CHUNK] * jnp.exp(b_last - b_c)).astype(BF16)
        dec_rows.append(jnp.exp(b_last))
    dec_all = jnp.concatenate(
        dec_rows + [jnp.zeros((LANES - NCH, GLA_DK), F32)], axis=0)
    dect_ref[...] = dec_all.T

    def r_piece(c0):
        rg_ref[:, c0:c0 + PIECE] = _silu(proj(IN_R + c0, IN_R + c0 + PIECE))

    def u_piece(c0):
        uext_ref[POOL_HALO:POOL_HALO + TS, c0:c0 + PIECE] = proj(IN_U + c0, IN_U + c0 + PIECE)

    def gp_piece(c0):
        gp_ref[:, c0:c0 + PIECE] = jax.nn.sigmoid(proj(IN_GP + c0, IN_GP + c0 + PIECE))

    def gg_piece(c0):
        gg_ref[:, c0:c0 + PIECE] = jax.nn.sigmoid(proj(IN_GG + c0, IN_GG + c0 + PIECE))

    pieces = ([functools.partial(r_piece, c0) for c0 in range(0, GLA_DV, PIECE)]
              + [functools.partial(u_piece, c0) for c0 in range(0, POOL_WIDTH, PIECE)]
              + [functools.partial(gp_piece, c0) for c0 in range(0, D_MODEL, PIECE)])

    row = lax.broadcasted_iota(jnp.int32, (CHUNK, CHUNK), 0)
    col = lax.broadcasted_iota(jnp.int32, (CHUNK, CHUNK), 1)
    causal = col <= row
    tn = (((0,), (0,)), ((), ()))
    n_steps = NCH * GLA_HEADS
    step = 0
    for c in range(NCH):
        r0 = c * CHUNK
        for hd in range(GLA_HEADS):
            k0 = hd * GLA_HK
            v0 = hd * GLA_HV
            q_c = qt_ref[r0:r0 + CHUNK, k0:k0 + GLA_HK]
            k_c = kt_ref[r0:r0 + CHUNK, k0:k0 + GLA_HK]
            ke_c = kend_ref[r0:r0 + CHUNK, k0:k0 + GLA_HK]
            v_c = v_ref[r0:r0 + CHUNK, v0:v0 + GLA_HV]
            st = state_ref[hd]
            scores = lax.dot_general(q_c, k_c, nt, preferred_element_type=F32)
            scores = jnp.where(causal, scores, 0.0).astype(BF16)
            lhs = jnp.concatenate([q_c, scores], axis=1)
            rhs = jnp.concatenate([st.astype(BF16), v_c], axis=0)
            og_ref[r0:r0 + CHUNK, v0:v0 + GLA_HV] = jnp.dot(lhs, rhs, preferred_element_type=F32)
            kv = lax.dot_general(ke_c, v_c, tn, preferred_element_type=F32)
            state_ref[hd] = st * dect_ref[k0:k0 + GLA_HK, c:c + 1] + kv
            step += 1
            while pieces and len(pieces) * n_steps > (n_steps - step) * N_PIECES:
                pieces.pop(0)()
    while pieces:
        pieces.pop(0)()

    o_parts = []
    for hd in range(GLA_HEADS):
        v0 = hd * GLA_HV
        o_h = og_ref[:, v0:v0 + GLA_HV]
        mu = jnp.mean(o_h, axis=-1, keepdims=True)
        dlt = o_h - mu
        var = jnp.mean(dlt * dlt, axis=-1, keepdims=True)
        o_parts.append((dlt * lax.rsqrt(var + LN_EPS) * gng_ref[0, :, v0:v0 + GLA_HV]
                        * rg_ref[:, v0:v0 + GLA_HV]).astype(BF16))
        gg_piece(hd * PIECE)
    o_n = jnp.concatenate(o_parts, axis=-1)
    y_gla = jnp.dot(o_n, wpg_ref[...], preferred_element_type=F32)

    pos = s * TS + lax.broadcasted_iota(jnp.int32, (TS, POOL_GW), 0)
    p_parts = []
    for gi, w in enumerate(POOL_WINDOWS):
        c0 = gi * POOL_GW
        u_g = uext_ref[POOL_HALO:POOL_HALO + TS, c0:c0 + POOL_GW]
        wsum = u_g
        for sft in range(1, w):
            wsum = wsum + uext_ref[POOL_HALO - sft:POOL_HALO - sft + TS, c0:c0 + POOL_GW]
        cnt = jnp.minimum(pos + 1, w).astype(F32)
        p_g = (wsum / cnt - u_g).astype(BF16)
        p_parts.append(jnp.dot(p_g, wpool_ref[0, gi], preferred_element_type=F32))
    p = (jnp.concatenate(p_parts, axis=-1) * pscale_ref[0]).astype(BF16)
    y_pool = jnp.dot(p, wpp_ref[...], preferred_element_type=F32)

    merged = gg_ref[...] * y_gla + gp_ref[...] * y_pool
    y = jnp.dot(merged.astype(BF16), wout_ref[...], preferred_element_type=F32)
    t = ALPHA * x + (1.0 + gt) * y
    o_ref[0] = _layer_norm(t, lng_ref[0, 1:2, :], lnb_ref[0, 1:2, :])


def _mixer_call(x, mod, big_weights, small_operands, *, layer):
    B, S, D = x.shape
    TS = MIX_TS
    return pl.pallas_call(
        functools.partial(_mixer_kernel, layer=layer),
        grid=(B, S // TS),
        in_specs=[
            pl.BlockSpec((1, TS, D), lambda b, s: (b, s, 0)),
            pl.BlockSpec((1, 1, N_MOD, D), lambda b, s: (layer, b, 0, 0)),
        ] + [pl.BlockSpec(memory_space=pl.ANY) for _ in big_weights]
          + [_layer_spec(a, layer) for a in small_operands],
        out_specs=pl.BlockSpec((1, TS, D), lambda b, s: (b, s, 0)),
        out_shape=jax.ShapeDtypeStruct((B, S, D), F32),
        scratch_shapes=[
            pltpu.VMEM((GLA_HEADS, GLA_HK, GLA_HV), F32),
            pltpu.VMEM((POOL_HALO + TS, POOL_WIDTH), F32),
            pltpu.VMEM((TS, GLA_DK), BF16),
            pltpu.VMEM((TS, GLA_DK), BF16),
            pltpu.VMEM((TS, GLA_DK), BF16),
            pltpu.VMEM((TS, GLA_DV), BF16),
            pltpu.VMEM((GLA_DK, LANES), F32),
            pltpu.VMEM((TS, GLA_DV), F32),
            pltpu.VMEM((TS, GLA_DV), F32),
            pltpu.VMEM((TS, D_MODEL), F32),
            pltpu.VMEM((TS, D_MODEL), F32),
            pltpu.VMEM((IN_WIDTH, D_MODEL), BF16),
            pltpu.VMEM((GLA_DV, D_MODEL), BF16),
            pltpu.VMEM((POOL_WIDTH, D_MODEL), BF16),
            pltpu.VMEM((D_MODEL, D_MODEL), BF16),
            pltpu.VMEM((2, MIX_STAGE_ROWS, D_MODEL), F32),
            pltpu.SemaphoreType.DMA((2,)),
        ],
        compiler_params=pltpu.CompilerParams(
            dimension_semantics=("arbitrary", "arbitrary"),
            vmem_limit_bytes=VMEM_LIMIT),
        name="mixer",
    )(x, mod, *big_weights, *small_operands)


def _mixer_slab_operands(w_alpha, b_alpha, gla_norm_g, w_pool, pool_scale, ln_g, ln_b):
    L = w_alpha.shape[0]
    w_al = jnp.pad(w_alpha, ((0, 0), (0, LANES - GLA_RANK), (0, 0))).astype(BF16)
    return (w_al, b_alpha.reshape(L, 1, GLA_DK), gla_norm_g.reshape(L, 1, GLA_DV),
            w_pool.astype(BF16), pool_scale.reshape(L, 1, POOL_WIDTH), ln_g, ln_b)


def kernel(x, c, w_ada, b_ada, ffn1_up, ffn1_down, w_in, w_alpha, b_alpha, gla_norm_g,
           w_proj_gla, w_pool, pool_scale, w_proj_pool, w_out, ffn2_up, ffn2_down, ln_g, ln_b):
    B, S, D = x.shape
    L = w_ada.shape[0]
    c_pad = jnp.pad(c, ((0, 8 - B), (0, 0)))
    mod = _ada_call(c_pad, w_ada, b_ada)[:, :B].reshape(L, B, N_MOD, D)
    mix_big = (jnp.swapaxes(w_in, 1, 2), w_proj_gla, w_proj_pool, w_out)
    mix_small = _mixer_slab_operands(w_alpha, b_alpha, gla_norm_g, w_pool, pool_scale, ln_g, ln_b)
    x2 = x.reshape(B * S, D)
    for l in range(L):
        x2 = _ffn_call(x2, mod, ffn1_up, ffn1_down, ln_g, ln_b, layer=l, mod_base=0, ln_row=0,
                       seq=S)
        x3 = _mixer_call(x2.reshape(B, S, D), mod, mix_big, mix_small, layer=l)
        x2 = _ffn_call(x3.reshape(B * S, D), mod, ffn2_up, ffn2_down, ln_g, ln_b, layer=l,
                       mod_base=6, ln_row=2, seq=S)
    return x2.reshape(B, S, D)
```

```python
import functools

import jax
import jax.numpy as jnp
from jax import lax
from jax.experimental import pallas as pl
from jax.experimental.pallas import tpu as pltpu

F32 = jnp.float32
BF16 = jnp.bfloat16

D_MODEL = 1024
DEPTH = 4
GLA_HEADS = 4
GLA_DK = 512
GLA_DV = 1024
GLA_HK = 128
GLA_HV = 256
GLA_RANK = 16
GLA_TAU = 16.0
CHUNK = 64
POOL_WIDTH = 512
POOL_WINDOWS = (2, 4, 8, 16)
POOL_GW = 128
POOL_HALO = 16
D_FF = 2816
N_MOD = 9
ALPHA = (2 * DEPTH) ** 0.25
LN_EPS = 1e-5
LANES = 128

FFN_TM = 512
FFN_TF = 256
MIX_TS = 512
LN_ROWS = 64
STAGE_COLS = 512
STAGE_ROWS = 256
MIX_STAGE_ROWS = 512
CUM_ROWS = 256
ADA_TN = 2304
VMEM_LIMIT = 56 * 1024 * 1024

IN_Q = 0
IN_K = IN_Q + GLA_DK
IN_V = IN_K + GLA_DK
IN_R = IN_V + GLA_DV
IN_A = IN_R + GLA_DV
IN_U = IN_A + GLA_RANK
IN_GG = IN_U + POOL_WIDTH
IN_GP = IN_GG + D_MODEL
IN_WIDTH = IN_GP + D_MODEL
PIECE = 256
PIECES_AHEAD = 2
N_PIECES = (GLA_DV + POOL_WIDTH + D_MODEL) // PIECE
CUM_TERMS = 2


def _layer_norm(t, g, b):
    mu = jnp.mean(t, axis=-1, keepdims=True)
    d = t - mu
    var = jnp.mean(d * d, axis=-1, keepdims=True)
    return d * lax.rsqrt(var + LN_EPS) * g + b


def _silu(t):
    return t * jax.nn.sigmoid(t)


def _zero_bits_of(v):
    bits = lax.bitcast_convert_type(v, jnp.uint32)
    acc = None
    for r0 in range(0, v.shape[0], 8):
        for c0 in range(0, v.shape[1], LANES):
            tile = bits[r0:r0 + 8, c0:c0 + LANES]
            acc = tile if acc is None else acc | tile
    return lax.shift_right_logical(lax.shift_right_logical(acc, jnp.uint32(16)), jnp.uint32(16))


def _after(v, zero_bits):
    reps = v.shape[0] // zero_bits.shape[0]
    z = jnp.concatenate([zero_bits] * reps, axis=0) if reps > 1 else zero_bits
    return jnp.where(z == 0, v, 0.0)


def _stage_bf16(windows, stage_ref, sems):
    def copy(j):
        src, _ = windows[j]
        slot = stage_ref.at[j % 2, 0:src.shape[0], 0:src.shape[1]]
        return pltpu.make_async_copy(src, slot, sems.at[j % 2])

    copy(0).start()
    for j, (src, dst) in enumerate(windows):
        if j + 1 < len(windows):
            copy(j + 1).start()
        copy(j).wait()
        dst[...] = stage_ref[j % 2, 0:src.shape[0], 0:src.shape[1]].astype(BF16)


def _col_windows(src, dst, width):
    return [(src.at[:, pl.ds(c0, STAGE_COLS)], dst.at[:, pl.ds(c0, STAGE_COLS)])
            for c0 in range(0, width, STAGE_COLS)]


def _row_windows(src, dst, n_rows, step):
    return [(src.at[pl.ds(r0, min(step, n_rows - r0)), :], dst.at[pl.ds(r0, min(step, n_rows - r0)), :])
            for r0 in range(0, n_rows, step)]


def _layer_spec(a, layer):
    nd = a.ndim
    return pl.BlockSpec((1,) + a.shape[1:], lambda *_: (layer,) + (0,) * (nd - 1),
                        pipeline_mode=pl.Buffered(1))


def _ada_kernel(c_ref, w_ref, b_ref, o_ref):
    c_act = _silu(c_ref[...]).astype(BF16)
    w = w_ref[0].astype(BF16)
    o_ref[0] = jnp.dot(c_act, w, preferred_element_type=F32) + b_ref[0]


def _ada_call(c_pad, w_ada, b_ada):
    L, D, N = w_ada.shape
    rows = c_pad.shape[0]
    return pl.pallas_call(
        _ada_kernel,
        grid=(L, N // ADA_TN),
        in_specs=[
            pl.BlockSpec((rows, D), lambda l, n: (0, 0)),
            pl.BlockSpec((1, D, ADA_TN), lambda l, n: (l, 0, n)),
            pl.BlockSpec((1, 1, ADA_TN), lambda l, n: (l, 0, n)),
        ],
        out_specs=pl.BlockSpec((1, rows, ADA_TN), lambda l, n: (l, 0, n)),
        out_shape=jax.ShapeDtypeStruct((L, rows, N), F32),
        compiler_params=pltpu.CompilerParams(
            dimension_semantics=("arbitrary", "arbitrary"),
            vmem_limit_bytes=VMEM_LIMIT),
        name="ada_mod",
    )(c_pad, w_ada, b_ada.reshape(L, 1, N))


def _ffn_kernel(xp_ref, x_ref, modp_ref, mod_ref, wup_hbm, wdn_hbm, lng_ref, lnb_ref,
                o_ref, a_ref, y_ref, wup_ref, wdn_ref, upstage_ref, dnstage_ref, up_sems, dn_sems,
                *, layer, mod_base, ln_row, n_tiles):
    i = pl.program_id(0)

    def post_norm(rb):
        rows = pl.ds(rb * LN_ROWS, LN_ROWS)
        gt = modp_ref[0, 0, mod_base + 2:mod_base + 3, :]
        t = ALPHA * xp_ref[rows, :] + (0.5 * (1.0 + gt)) * y_ref[rows, :]
        out = _layer_norm(t, lng_ref[0, ln_row:ln_row + 1, :], lnb_ref[0, ln_row:ln_row + 1, :])
        o_ref[rows, :] = out
        return out

    @pl.when(i == 0)
    def _():
        y_ref[...] = jnp.zeros_like(y_ref)
        _stage_bf16(_col_windows(wup_hbm.at[layer], wup_ref, 2 * D_FF), upstage_ref, up_sems)
        _stage_bf16(_row_windows(wdn_hbm.at[layer], wdn_ref, D_FF, STAGE_ROWS), dnstage_ref, dn_sems)

    @pl.when(i < n_tiles)
    def _():
        sh = mod_ref[0, 0, mod_base:mod_base + 1, :]
        sc = mod_ref[0, 0, mod_base + 1:mod_base + 2, :]
        h = (x_ref[...] * (1.0 + sc) + sh).astype(BF16)
        anchor = None
        for ci, c0 in enumerate(range(0, D_FF, FFN_TF)):
            g = jnp.dot(h, wup_ref[:, c0:c0 + FFN_TF], preferred_element_type=F32)
            u = jnp.dot(h, wup_ref[:, D_FF + c0:D_FF + c0 + FFN_TF], preferred_element_type=F32)
            act = _silu(g) * u
            a_ref[:, c0:c0 + FFN_TF] = act.astype(BF16)
            if anchor is not None:
                a_ref[0:16, c0:c0 + LANES] = _after(act[0:16, 0:LANES], anchor).astype(BF16)
            anchor = _zero_bits_of(post_norm(ci)) if ci < FFN_TM // LN_ROWS else None
        y_ref[...] = jnp.dot(a_ref[...], wdn_ref[...], preferred_element_type=F32)

    @pl.when(i == n_tiles)
    def _():
        for rb in range(FFN_TM // LN_ROWS):
            post_norm(rb)


def _ffn_call(x2, mod, w_up, w_down, ln_g, ln_b, *, layer, mod_base, ln_row, seq):
    M, D = x2.shape
    n_tiles = M // FFN_TM
    tiles_per_seq = seq // FFN_TM
    assert D_FF // FFN_TF >= FFN_TM // LN_ROWS

    def cur(i):
        return jnp.minimum(i, n_tiles - 1)

    def prev(i):
        return jnp.maximum(i - 1, 0)

    return pl.pallas_call(
        functools.partial(_ffn_kernel, layer=layer, mod_base=mod_base, ln_row=ln_row,
                          n_tiles=n_tiles),
        grid=(n_tiles + 1,),
        in_specs=[
            pl.BlockSpec((FFN_TM, D), lambda i: (prev(i), 0)),
            pl.BlockSpec((FFN_TM, D), lambda i: (cur(i), 0)),
            pl.BlockSpec((1, 1, N_MOD, D), lambda i: (layer, prev(i) // tiles_per_seq, 0, 0)),
            pl.BlockSpec((1, 1, N_MOD, D), lambda i: (layer, cur(i) // tiles_per_seq, 0, 0)),
            pl.BlockSpec(memory_space=pl.ANY),
            pl.BlockSpec(memory_space=pl.ANY),
            _layer_spec(ln_g, layer),
            _layer_spec(ln_b, layer),
        ],
        out_specs=pl.BlockSpec((FFN_TM, D), lambda i: (prev(i), 0)),
        out_shape=jax.ShapeDtypeStruct((M, D), F32),
        scratch_shapes=[
            pltpu.VMEM((FFN_TM, D_FF), BF16),
            pltpu.VMEM((FFN_TM, D), F32),
            pltpu.VMEM((D, 2 * D_FF), BF16),
            pltpu.VMEM((D_FF, D), BF16),
            pltpu.VMEM((2, D, STAGE_COLS), F32),
            pltpu.VMEM((2, STAGE_ROWS, D), F32),
            pltpu.SemaphoreType.DMA((2,)),
            pltpu.SemaphoreType.DMA((2,)),
        ],
        compiler_params=pltpu.CompilerParams(
            dimension_semantics=("arbitrary",),
            vmem_limit_bytes=VMEM_LIMIT),
        name="ffn",
    )(x2, x2, mod, mod, w_up, w_down, ln_g, ln_b)


def _split_bf16(t, parts):
    out = []
    r = t
    for _ in range(parts):
        p = r.astype(BF16)
        out.append(p)
        r = r - p.astype(F32)
    return out


def _mixer_tile(s, post_norm, x_ref, mod_ref, walpha_ref, balpha_ref, gng_ref, wpool_ref, pscale_ref,
                state_ref, uext_ref, qt_ref, kt_ref, kend_ref, v_ref, dect_ref, og_ref,
                rg_ref, gg_ref, gp_ref, wint_ref, wpg_ref, wpp_ref, wout_ref, y_ref):
    TS = MIX_TS
    NCH = TS // CHUNK

    @pl.when(s == 0)
    def _():
        state_ref[...] = jnp.zeros_like(state_ref)
        uext_ref[0:POOL_HALO, :] = jnp.zeros((POOL_HALO, POOL_WIDTH), F32)

    @pl.when(s > 0)
    def _():
        uext_ref[0:POOL_HALO, :] = uext_ref[TS:TS + POOL_HALO, :]

    x = x_ref[0]
    sh = mod_ref[0, 0, 3:4, :]
    sc = mod_ref[0, 0, 4:5, :]
    h = (x * (1.0 + sc) + sh).astype(BF16)

    nt = (((1,), (1,)), ((), ()))

    def proj(r0, r1):
        return lax.dot_general(h, wint_ref[r0:r1, :], nt, preferred_element_type=F32)

    a_wide = proj(IN_A, IN_A + LANES)
    v_ref[:, 0:PIECE] = proj(IN_V, IN_V + PIECE).astype(BF16)
    lane = lax.broadcasted_iota(jnp.int32, a_wide.shape, 1)
    a_lr = jnp.where(lane < GLA_RANK, a_wide, 0.0).astype(BF16)
    pre = jnp.dot(a_lr, walpha_ref[0], preferred_element_type=F32) + balpha_ref[0]
    log_a = (jnp.minimum(pre, 0.0) - jnp.log(1.0 + jnp.exp(-jnp.abs(pre)))) / GLA_TAU

    v_ref[:, PIECE:GLA_DV] = proj(IN_V + PIECE, IN_R).astype(BF16)

    ri = lax.broadcasted_iota(jnp.int32, (CUM_ROWS, CUM_ROWS), 0)
    ci = lax.broadcasted_iota(jnp.int32, (CUM_ROWS, CUM_ROWS), 1)
    tri = jnp.where((ri // CHUNK == ci // CHUNK) & (ci <= ri), 1.0, 0.0).astype(BF16)
    b_parts = []
    for g0 in range(0, TS, CUM_ROWS):
        acc = None
        for part in _split_bf16(log_a[g0:g0 + CUM_ROWS], CUM_TERMS):
            t = jnp.dot(tri, part, preferred_element_type=F32)
            acc = t if acc is None else acc + t
        b_parts.append(acc)
    b = jnp.concatenate(b_parts, axis=0)

    q = proj(IN_Q, IN_K) * (GLA_HK ** -0.5)
    k = proj(IN_K, IN_V)
    qt_ref[...] = (q * jnp.exp(b)).astype(BF16)
    kt_ref[...] = (k * jnp.exp(-b)).astype(BF16)
    dec_rows = []
    for c in range(NCH):
        r0 = c * CHUNK
        b_c = b[r0:r0 + CHUNK]
        b_last = b_c[CHUNK - 1:CHUNK, :]
        kend_ref[r0:r0 + CHUNK, :] = (k[r0:r0 + CHUNK] * jnp.exp(b_last - b_c)).astype(BF16)
        dec_rows.append(jnp.exp(b_last))
    dec_all = jnp.concatenate(
        dec_rows + [jnp.zeros((LANES - NCH, GLA_DK), F32)], axis=0)
    dect_ref[...] = dec_all.T

    def r_piece(c0):
        rg_ref[:, c0:c0 + PIECE] = _silu(proj(IN_R + c0, IN_R + c0 + PIECE))

    def u_piece(c0):
        uext_ref[POOL_HALO:POOL_HALO + TS, c0:c0 + PIECE] = proj(IN_U + c0, IN_U + c0 + PIECE)

    def gp_piece(c0):
        gp_ref[:, c0:c0 + PIECE] = jax.nn.sigmoid(proj(IN_GP + c0, IN_GP + c0 + PIECE))

    def gg_piece(c0):
        gg_ref[:, c0:c0 + PIECE] = jax.nn.sigmoid(proj(IN_GG + c0, IN_GG + c0 + PIECE))

    pieces = ([functools.partial(r_piece, c0) for c0 in range(0, GLA_DV, PIECE)]
              + [functools.partial(u_piece, c0) for c0 in range(0, POOL_WIDTH, PIECE)]
              + [functools.partial(gp_piece, c0) for c0 in range(0, D_MODEL, PIECE)])

    row = lax.broadcasted_iota(jnp.int32, (CHUNK, CHUNK), 0)
    col = lax.broadcasted_iota(jnp.int32, (CHUNK, CHUNK), 1)
    causal = col <= row
    tn = (((0,), (0,)), ((), ()))
    n_steps = NCH * GLA_HEADS
    step = 0

    def chunk_scores(c):
        out = []
        for hd in range(GLA_HEADS):
            rows, cols = slice(c * CHUNK, (c + 1) * CHUNK), slice(hd * GLA_HK, (hd + 1) * GLA_HK)
            sc = lax.dot_general(qt_ref[rows, cols], kt_ref[rows, cols], nt,
                                 preferred_element_type=F32)
            out.append(jnp.where(causal, sc, 0.0).astype(BF16))
        return out

    for _ in range(PIECES_AHEAD):
        pieces.pop(0)()
    scores_next = chunk_scores(0)
    for c in range(NCH):
        r0 = c * CHUNK
        scores_cur = scores_next
        if c + 1 < NCH:
            scores_next = chunk_scores(c + 1)
        anchor = _zero_bits_of(post_norm(c)) if c < TS // LN_ROWS else None
        for hd in range(GLA_HEADS):
            k0 = hd * GLA_HK
            v0 = hd * GLA_HV
            q_c = qt_ref[r0:r0 + CHUNK, k0:k0 + GLA_HK]
            ke_c = kend_ref[r0:r0 + CHUNK, k0:k0 + GLA_HK]
            v_c = v_ref[r0:r0 + CHUNK, v0:v0 + GLA_HV]
            st = state_ref[hd]
            scores = scores_cur[hd]
            lhs = jnp.concatenate([q_c, scores], axis=1)
            rhs = jnp.concatenate([st.astype(BF16), v_c], axis=0)
            og_ref[r0:r0 + CHUNK, v0:v0 + GLA_HV] = jnp.dot(lhs, rhs, preferred_element_type=F32)
            kv = lax.dot_general(ke_c, v_c, tn, preferred_element_type=F32)
            st_new = st * dect_ref[k0:k0 + GLA_HK, c:c + 1] + kv
            state_ref[hd] = st_new
            if anchor is not None and hd == GLA_HEADS - 1:
                state_ref[hd, 0:8, 0:LANES] = _after(st_new[0:8, 0:LANES], anchor)
            step += 1
            while pieces and len(pieces) * n_steps > (n_steps - step) * N_PIECES:
                pieces.pop(0)()
    while pieces:
        pieces.pop(0)()

    o_parts = []
    for hd in range(GLA_HEADS):
        v0 = hd * GLA_HV
        o_h = og_ref[:, v0:v0 + GLA_HV]
        mu = jnp.mean(o_h, axis=-1, keepdims=True)
        dlt = o_h - mu
        var = jnp.mean(dlt * dlt, axis=-1, keepdims=True)
        o_parts.append((dlt * lax.rsqrt(var + LN_EPS) * gng_ref[0, :, v0:v0 + GLA_HV]
                        * rg_ref[:, v0:v0 + GLA_HV]).astype(BF16))
        gg_piece(hd * PIECE)
    o_n = jnp.concatenate(o_parts, axis=-1)
    y_gla = jnp.dot(o_n, wpg_ref[...], preferred_element_type=F32)

    pos = s * TS + lax.broadcasted_iota(jnp.int32, (TS, POOL_GW), 0)
    p_parts = []
    for gi, w in enumerate(POOL_WINDOWS):
        c0 = gi * POOL_GW
        ext = uext_ref[:, c0:c0 + POOL_GW]
        span = 1
        while span < w:
            ext = ext + pltpu.roll(ext, span, axis=0)
            span *= 2
        wsum = ext[POOL_HALO:POOL_HALO + TS]
        u_g = uext_ref[POOL_HALO:POOL_HALO + TS, c0:c0 + POOL_GW]
        cnt = jnp.minimum(pos + 1, w).astype(F32)
        p_g = (wsum / cnt - u_g).astype(BF16)
        p_parts.append(jnp.dot(p_g, wpool_ref[0, gi], preferred_element_type=F32))
    p = (jnp.concatenate(p_parts, axis=-1) * pscale_ref[0]).astype(BF16)
    y_pool = jnp.dot(p, wpp_ref[...], preferred_element_type=F32)

    merged = gg_ref[...] * y_gla + gp_ref[...] * y_pool
    y_ref[...] = jnp.dot(merged.astype(BF16), wout_ref[...], preferred_element_type=F32)


def _mixer_kernel(xp_ref, x_ref, modp_ref, mod_ref, wint_hbm, wpg_hbm, wpp_hbm, wout_hbm,
                  walpha_ref, balpha_ref, gng_ref, wpool_ref, pscale_ref, lng_ref, lnb_ref,
                  o_ref,
                  state_ref, uext_ref, qt_ref, kt_ref, kend_ref, v_ref, dect_ref, og_ref,
                  rg_ref, gg_ref, gp_ref,
                  wint_ref, wpg_ref, wpp_ref, wout_ref, y_ref, stage_ref, sems,
                  *, layer, n_tiles, tiles_per_seq):
    i = pl.program_id(0)

    def post_norm(rb):
        rows = pl.ds(rb * LN_ROWS, LN_ROWS)
        gt = modp_ref[0, 0, 5:6, :]
        t = ALPHA * xp_ref[0, rows, :] + (1.0 + gt) * y_ref[rows, :]
        out = _layer_norm(t, lng_ref[0, 1:2, :], lnb_ref[0, 1:2, :])
        o_ref[0, rows, :] = out
        return out

    @pl.when(i == 0)
    def _():
        y_ref[...] = jnp.zeros_like(y_ref)
        _stage_bf16(
            _row_windows(wint_hbm.at[layer], wint_ref, IN_WIDTH, MIX_STAGE_ROWS)
            + _row_windows(wpg_hbm.at[layer], wpg_ref, GLA_DV, MIX_STAGE_ROWS)
            + _row_windows(wout_hbm.at[layer], wout_ref, D_MODEL, MIX_STAGE_ROWS)
            + _row_windows(wpp_hbm.at[layer], wpp_ref, POOL_WIDTH, MIX_STAGE_ROWS),
            stage_ref, sems)

    @pl.when(i < n_tiles)
    def _():
        _mixer_tile(i % tiles_per_seq, post_norm, x_ref, mod_ref, walpha_ref, balpha_ref, gng_ref,
                    wpool_ref, pscale_ref, state_ref, uext_ref, qt_ref, kt_ref, kend_ref, v_ref,
                    dect_ref, og_ref, rg_ref, gg_ref, gp_ref, wint_ref, wpg_ref, wpp_ref, wout_ref,
                    y_ref)

    @pl.when(i == n_tiles)
    def _():
        for rb in range(MIX_TS // LN_ROWS):
            post_norm(rb)


def _mixer_call(x, mod, big_weights, small_operands, *, layer):
    B, S, D = x.shape
    TS = MIX_TS
    tiles_per_seq = S // TS
    n_tiles = B * tiles_per_seq

    def cur(i):
        return jnp.minimum(i, n_tiles - 1)

    def prev(i):
        return jnp.maximum(i - 1, 0)

    def x_map(tile):
        return lambda i: (tile(i) // tiles_per_seq, tile(i) % tiles_per_seq, 0)

    def mod_map(tile):
        return lambda i: (layer, tile(i) // tiles_per_seq, 0, 0)

    return pl.pallas_call(
        functools.partial(_mixer_kernel, layer=layer, n_tiles=n_tiles, tiles_per_seq=tiles_per_seq),
        grid=(n_tiles + 1,),
        in_specs=[
            pl.BlockSpec((1, TS, D), x_map(prev)),
            pl.BlockSpec((1, TS, D), x_map(cur)),
            pl.BlockSpec((1, 1, N_MOD, D), mod_map(prev)),
            pl.BlockSpec((1, 1, N_MOD, D), mod_map(cur)),
        ] + [pl.BlockSpec(memory_space=pl.ANY) for _ in big_weights]
          + [_layer_spec(a, layer) for a in small_operands],
        out_specs=pl.BlockSpec((1, TS, D), x_map(prev)),
        out_shape=jax.ShapeDtypeStruct((B, S, D), F32),
        scratch_shapes=[
            pltpu.VMEM((GLA_HEADS, GLA_HK, GLA_HV), F32),
            pltpu.VMEM((POOL_HALO + TS, POOL_WIDTH), F32),
            pltpu.VMEM((TS, GLA_DK), BF16),
            pltpu.VMEM((TS, GLA_DK), BF16),
            pltpu.VMEM((TS, GLA_DK), BF16),
            pltpu.VMEM((TS, GLA_DV), BF16),
            pltpu.VMEM((GLA_DK, LANES), F32),
            pltpu.VMEM((TS, GLA_DV), F32),
            pltpu.VMEM((TS, GLA_DV), F32),
            pltpu.VMEM((TS, D_MODEL), F32),
            pltpu.VMEM((TS, D_MODEL), F32),
            pltpu.VMEM((IN_WIDTH, D_MODEL), BF16),
            pltpu.VMEM((GLA_DV, D_MODEL), BF16),
            pltpu.VMEM((POOL_WIDTH, D_MODEL), BF16),
            pltpu.VMEM((D_MODEL, D_MODEL), BF16),
            pltpu.VMEM((TS, D_MODEL), F32),
            pltpu.VMEM((2, MIX_STAGE_ROWS, D_MODEL), F32),
            pltpu.SemaphoreType.DMA((2,)),
        ],
        compiler_params=pltpu.CompilerParams(
            dimension_semantics=("arbitrary",),
            vmem_limit_bytes=VMEM_LIMIT),
        name="mixer",
    )(x, x, mod, mod, *big_weights, *small_operands)


def _mixer_slab_operands(w_alpha, b_alpha, gla_norm_g, w_pool, pool_scale, ln_g, ln_b):
    L = w_alpha.shape[0]
    w_al = jnp.pad(w_alpha, ((0, 0), (0, LANES - GLA_RANK), (0, 0))).astype(BF16)
    return (w_al, b_alpha.reshape(L, 1, GLA_DK), gla_norm_g.reshape(L, 1, GLA_DV),
            w_pool.astype(BF16), pool_scale.reshape(L, 1, POOL_WIDTH), ln_g, ln_b)


def kernel(x, c, w_ada, b_ada, ffn1_up, ffn1_down, w_in, w_alpha, b_alpha, gla_norm_g,
           w_proj_gla, w_pool, pool_scale, w_proj_pool, w_out, ffn2_up, ffn2_down, ln_g, ln_b):
    B, S, D = x.shape
    L = w_ada.shape[0]
    c_pad = jnp.pad(c, ((0, 8 - B), (0, 0)))
    mod = _ada_call(c_pad, w_ada, b_ada)[:, :B].reshape(L, B, N_MOD, D)
    mix_big = (jnp.swapaxes(w_in, 1, 2), w_proj_gla, w_proj_pool, w_out)
    mix_small = _mixer_slab_operands(w_alpha, b_alpha, gla_norm_g, w_pool, pool_scale, ln_g, ln_b)
    x2 = x.reshape(B * S, D)
    for l in range(L):
        x2 = _ffn_call(x2, mod, ffn1_up, ffn1_down, ln_g, ln_b, layer=l, mod_base=0, ln_row=0,
                       seq=S)
        x3 = _mixer_call(x2.reshape(B, S, D), mod, mix_big, mix_small, layer=l)
        x2 = _ffn_call(x3.reshape(B * S, D), mod, ffn2_up, ffn2_down, ln_g, ln_b, layer=l,
                       mod_base=6, ln_row=2, seq=S)
    return x2.reshape(B, S, D)
```

```python
import functools

import jax
import jax.numpy as jnp
from jax import lax
from jax.experimental import pallas as pl
from jax.experimental.pallas import tpu as pltpu

F32 = jnp.float32
BF16 = jnp.bfloat16

D_MODEL = 1024
DEPTH = 4
GLA_HEADS = 4
GLA_DK = 512
GLA_DV = 1024
GLA_HK = 128
GLA_HV = 256
GLA_RANK = 16
GLA_TAU = 16.0
CHUNK = 64
POOL_WIDTH = 512
POOL_WINDOWS = (2, 4, 8, 16)
POOL_GW = 128
POOL_HALO = 16
D_FF = 2816
N_MOD = 9
ALPHA = (2 * DEPTH) ** 0.25
LN_EPS = 1e-5
LANES = 128

FFN_TM = 512
FFN_TF = 256
MIX_TS = 512
LN_ROWS = 64
MIX_STAGE_ROWS = 512
CUM_ROWS = 256
ADA_TN = 2304
VMEM_LIMIT = 56 * 1024 * 1024

IN_Q = 0
IN_K = IN_Q + GLA_DK
IN_V = IN_K + GLA_DK
IN_R = IN_V + GLA_DV
IN_A = IN_R + GLA_DV
IN_U = IN_A + GLA_RANK
IN_GG = IN_U + POOL_WIDTH
IN_GP = IN_GG + D_MODEL
IN_WIDTH = IN_GP + D_MODEL
PIECE = 256
PIECES_AHEAD = 2
N_PIECES = (GLA_DV + POOL_WIDTH + D_MODEL) // PIECE
CUM_TERMS = 2


def _layer_norm(t, g, b):
    mu = jnp.mean(t, axis=-1, keepdims=True)
    d = t - mu
    var = jnp.mean(d * d, axis=-1, keepdims=True)
    return d * lax.rsqrt(var + LN_EPS) * g + b


def _silu(t):
    return t * jax.nn.sigmoid(t)


def _zero_bits_of(v):
    bits = lax.bitcast_convert_type(v, jnp.uint32)
    acc = None
    for r0 in range(0, v.shape[0], 8):
        for c0 in range(0, v.shape[1], LANES):
            tile = bits[r0:r0 + 8, c0:c0 + LANES]
            acc = tile if acc is None else acc | tile
    return lax.shift_right_logical(lax.shift_right_logical(acc, jnp.uint32(16)), jnp.uint32(16))


def _after(v, zero_bits):
    reps = v.shape[0] // zero_bits.shape[0]
    z = jnp.concatenate([zero_bits] * reps, axis=0) if reps > 1 else zero_bits
    return jnp.where(z == 0, v, 0.0)


def _window_stream(windows, stage_ref, sems):
    def copy(j):
        src, _ = windows[j]
        slot = stage_ref.at[j % 2, 0:src.shape[0], 0:src.shape[1]]
        return pltpu.make_async_copy(src, slot, sems.at[j % 2])

    def start(j):
        if j < len(windows):
            copy(j).start()

    def wait(j):
        copy(j).wait()

    def convert(j):
        src, dst = windows[j]
        dst[...] = stage_ref[j % 2, 0:src.shape[0], 0:src.shape[1]].astype(BF16)

    return start, wait, convert


def _stage_bf16(windows, stage_ref, sems):
    start, wait, convert = _window_stream(windows, stage_ref, sems)
    start(0)
    for j in range(len(windows)):
        start(j + 1)
        wait(j)
        convert(j)


def _row_windows(src, dst, n_rows, step):
    return [(src.at[pl.ds(r0, min(step, n_rows - r0)), :], dst.at[pl.ds(r0, min(step, n_rows - r0)), :])
            for r0 in range(0, n_rows, step)]


def _layer_spec(a, layer):
    nd = a.ndim
    return pl.BlockSpec((1,) + a.shape[1:], lambda *_: (layer,) + (0,) * (nd - 1),
                        pipeline_mode=pl.Buffered(1))


def _ada_kernel(c_ref, w_ref, b_ref, o_ref):
    c_act = _silu(c_ref[...]).astype(BF16)
    w = w_ref[0].astype(BF16)
    o_ref[0] = jnp.dot(c_act, w, preferred_element_type=F32) + b_ref[0]


def _ada_call(c_pad, w_ada, b_ada):
    L, D, N = w_ada.shape
    rows = c_pad.shape[0]
    return pl.pallas_call(
        _ada_kernel,
        grid=(L, N // ADA_TN),
        in_specs=[
            pl.BlockSpec((rows, D), lambda l, n: (0, 0)),
            pl.BlockSpec((1, D, ADA_TN), lambda l, n: (l, 0, n)),
            pl.BlockSpec((1, 1, ADA_TN), lambda l, n: (l, 0, n)),
        ],
        out_specs=pl.BlockSpec((1, rows, ADA_TN), lambda l, n: (l, 0, n)),
        out_shape=jax.ShapeDtypeStruct((L, rows, N), F32),
        compiler_params=pltpu.CompilerParams(
            dimension_semantics=("arbitrary", "arbitrary"),
            vmem_limit_bytes=VMEM_LIMIT),
        name="ada_mod",
    )(c_pad, w_ada, b_ada.reshape(L, 1, N))


def _ffn_kernel(xp_ref, x_ref, modp_ref, mod_ref, wup_hbm, wdn_hbm, lng_ref, lnb_ref,
                o_ref, a_ref, y_ref, wup_ref, wdn_ref, gstage_ref, ustage_ref, dstage_ref,
                g_sems, u_sems, d_sems, *, layer, mod_base, ln_row, n_tiles):
    i = pl.program_id(0)
    chunks = range(0, D_FF, FFN_TF)
    streams = [
        _window_stream([(wup_hbm.at[layer, :, pl.ds(c0, FFN_TF)], wup_ref.at[:, pl.ds(c0, FFN_TF)])
                        for c0 in chunks], gstage_ref, g_sems),
        _window_stream([(wup_hbm.at[layer, :, pl.ds(D_FF + c0, FFN_TF)],
                         wup_ref.at[:, pl.ds(D_FF + c0, FFN_TF)]) for c0 in chunks],
                       ustage_ref, u_sems),
        _window_stream(_row_windows(wdn_hbm.at[layer], wdn_ref, D_FF, FFN_TF), dstage_ref, d_sems),
    ]

    def post_norm(rb):
        rows = pl.ds(rb * LN_ROWS, LN_ROWS)
        gt = modp_ref[0, 0, mod_base + 2:mod_base + 3, :]
        t = ALPHA * xp_ref[rows, :] + (0.5 * (1.0 + gt)) * y_ref[rows, :]
        out = _layer_norm(t, lng_ref[0, ln_row:ln_row + 1, :], lnb_ref[0, ln_row:ln_row + 1, :])
        o_ref[rows, :] = out
        return out

    def tile(stage_weights):
        if stage_weights:
            for start, _, _ in streams:
                start(0)
                start(1)
        sh = mod_ref[0, 0, mod_base:mod_base + 1, :]
        sc = mod_ref[0, 0, mod_base + 1:mod_base + 2, :]
        h = (x_ref[...] * (1.0 + sc) + sh).astype(BF16)
        anchor = None
        for ci, c0 in enumerate(chunks):
            if stage_weights:
                for _, wait, _ in streams:
                    wait(ci)
                for start, _, convert in streams:
                    convert(ci)
                    start(ci + 2)
            g = jnp.dot(h, wup_ref[:, c0:c0 + FFN_TF], preferred_element_type=F32)
            u = jnp.dot(h, wup_ref[:, D_FF + c0:D_FF + c0 + FFN_TF], preferred_element_type=F32)
            act = _silu(g) * u
            a_ref[:, c0:c0 + FFN_TF] = act.astype(BF16)
            if anchor is not None:
                a_ref[0:16, c0:c0 + LANES] = _after(act[0:16, 0:LANES], anchor).astype(BF16)
            anchor = _zero_bits_of(post_norm(ci)) if ci < FFN_TM // LN_ROWS else None
        y_ref[...] = jnp.dot(a_ref[...], wdn_ref[...], preferred_element_type=F32)

    @pl.when(i == 0)
    def _():
        y_ref[...] = jnp.zeros_like(y_ref)
        tile(stage_weights=True)

    @pl.when((i > 0) & (i < n_tiles))
    def _():
        tile(stage_weights=False)

    @pl.when(i == n_tiles)
    def _():
        for rb in range(FFN_TM // LN_ROWS):
            post_norm(rb)


def _ffn_call(x2, mod, w_up, w_down, ln_g, ln_b, *, layer, mod_base, ln_row, seq):
    M, D = x2.shape
    n_tiles = M // FFN_TM
    tiles_per_seq = seq // FFN_TM
    assert D_FF // FFN_TF >= FFN_TM // LN_ROWS

    def cur(i):
        return jnp.minimum(i, n_tiles - 1)

    def prev(i):
        return jnp.maximum(i - 1, 0)

    return pl.pallas_call(
        functools.partial(_ffn_kernel, layer=layer, mod_base=mod_base, ln_row=ln_row,
                          n_tiles=n_tiles),
        grid=(n_tiles + 1,),
        in_specs=[
            pl.BlockSpec((FFN_TM, D), lambda i: (prev(i), 0)),
            pl.BlockSpec((FFN_TM, D), lambda i: (cur(i), 0)),
            pl.BlockSpec((1, 1, N_MOD, D), lambda i: (layer, prev(i) // tiles_per_seq, 0, 0)),
            pl.BlockSpec((1, 1, N_MOD, D), lambda i: (layer, cur(i) // tiles_per_seq, 0, 0)),
            pl.BlockSpec(memory_space=pl.ANY),
            pl.BlockSpec(memory_space=pl.ANY),
            _layer_spec(ln_g, layer),
            _layer_spec(ln_b, layer),
        ],
        out_specs=pl.BlockSpec((FFN_TM, D), lambda i: (prev(i), 0)),
        out_shape=jax.ShapeDtypeStruct((M, D), F32),
        scratch_shapes=[
            pltpu.VMEM((FFN_TM, D_FF), BF16),
            pltpu.VMEM((FFN_TM, D), F32),
            pltpu.VMEM((D, 2 * D_FF), BF16),
            pltpu.VMEM((D_FF, D), BF16),
            pltpu.VMEM((2, D, FFN_TF), F32),
            pltpu.VMEM((2, D, FFN_TF), F32),
            pltpu.VMEM((2, FFN_TF, D), F32),
            pltpu.SemaphoreType.DMA((2,)),
            pltpu.SemaphoreType.DMA((2,)),
            pltpu.SemaphoreType.DMA((2,)),
        ],
        compiler_params=pltpu.CompilerParams(
            dimension_semantics=("arbitrary",),
            vmem_limit_bytes=VMEM_LIMIT),
        name="ffn",
    )(x2, x2, mod, mod, w_up, w_down, ln_g, ln_b)


def _split_bf16(t, parts):
    out = []
    r = t
    for _ in range(parts):
        p = r.astype(BF16)
        out.append(p)
        r = r - p.astype(F32)
    return out


def _mixer_tile(s, post_norm, x_ref, mod_ref, walpha_ref, balpha_ref, gng_ref, wpool_ref, pscale_ref,
                state_ref, uext_ref, qt_ref, kt_ref, kend_ref, v_ref, dect_ref, og_ref,
                rg_ref, gg_ref, gp_ref, wint_ref, wpg_ref, wpp_ref, wout_ref, y_ref):
    TS = MIX_TS
    NCH = TS // CHUNK

    @pl.when(s == 0)
    def _():
        state_ref[...] = jnp.zeros_like(state_ref)
        uext_ref[0:POOL_HALO, :] = jnp.zeros((POOL_HALO, POOL_WIDTH), F32)

    @pl.when(s > 0)
    def _():
        uext_ref[0:POOL_HALO, :] = uext_ref[TS:TS + POOL_HALO, :]

    x = x_ref[0]
    sh = mod_ref[0, 0, 3:4, :]
    sc = mod_ref[0, 0, 4:5, :]
    h = (x * (1.0 + sc) + sh).astype(BF16)

    nt = (((1,), (1,)), ((), ()))

    def proj(r0, r1):
        return lax.dot_general(h, wint_ref[r0:r1, :], nt, preferred_element_type=F32)

    a_wide = proj(IN_A, IN_A + LANES)
    v_ref[:, 0:PIECE] = proj(IN_V, IN_V + PIECE).astype(BF16)
    lane = lax.broadcasted_iota(jnp.int32, a_wide.shape, 1)
    a_lr = jnp.where(lane < GLA_RANK, a_wide, 0.0).astype(BF16)
    pre = jnp.dot(a_lr, walpha_ref[0], preferred_element_type=F32) + balpha_ref[0]
    log_a = (jnp.minimum(pre, 0.0) - jnp.log(1.0 + jnp.exp(-jnp.abs(pre)))) / GLA_TAU

    v_ref[:, PIECE:GLA_DV] = proj(IN_V + PIECE, IN_R).astype(BF16)

    ri = lax.broadcasted_iota(jnp.int32, (CUM_ROWS, CUM_ROWS), 0)
    ci = lax.broadcasted_iota(jnp.int32, (CUM_ROWS, CUM_ROWS), 1)
    tri = jnp.where((ri // CHUNK == ci // CHUNK) & (ci <= ri), 1.0, 0.0).astype(BF16)
    b_parts = []
    for g0 in range(0, TS, CUM_ROWS):
        acc = None
        for part in _split_bf16(log_a[g0:g0 + CUM_ROWS], CUM_TERMS):
            t = jnp.dot(tri, part, preferred_element_type=F32)
            acc = t if acc is None else acc + t
        b_parts.append(acc)
    b = jnp.concatenate(b_parts, axis=0)

    q = proj(IN_Q, IN_K) * (GLA_HK ** -0.5)
    k = proj(IN_K, IN_V)
    qt_ref[...] = (q * jnp.exp(b)).astype(BF16)
    kt_ref[...] = (k * jnp.exp(-b)).astype(BF16)
    dec_rows = []
    for c in range(NCH):
        r0 = c * CHUNK
        b_c = b[r0:r0 + CHUNK]
        b_last = b_c[CHUNK - 1:CHUNK, :]
        kend_ref[r0:r0 + CHUNK, :] = (k[r0:r0 + CHUNK] * jnp.exp(b_last - b_c)).astype(BF16)
        dec_rows.append(jnp.exp(b_last))
    dec_all = jnp.concatenate(
        dec_rows + [jnp.zeros((LANES - NCH, GLA_DK), F32)], axis=0)
    dect_ref[...] = dec_all.T

    def r_piece(c0):
        rg_ref[:, c0:c0 + PIECE] = _silu(proj(IN_R + c0, IN_R + c0 + PIECE))

    def u_piece(c0):
        uext_ref[POOL_HALO:POOL_HALO + TS, c0:c0 + PIECE] = proj(IN_U + c0, IN_U + c0 + PIECE)

    def gp_piece(c0):
        gp_ref[:, c0:c0 + PIECE] = jax.nn.sigmoid(proj(IN_GP + c0, IN_GP + c0 + PIECE))

    def gg_piece(c0):
        gg_ref[:, c0:c0 + PIECE] = jax.nn.sigmoid(proj(IN_GG + c0, IN_GG + c0 + PIECE))

    pieces = ([functools.partial(r_piece, c0) for c0 in range(0, GLA_DV, PIECE)]
              + [functools.partial(u_piece, c0) for c0 in range(0, POOL_WIDTH, PIECE)]
              + [functools.partial(gp_piece, c0) for c0 in range(0, D_MODEL, PIECE)])

    row = lax.broadcasted_iota(jnp.int32, (CHUNK, CHUNK), 0)
    col = lax.broadcasted_iota(jnp.int32, (CHUNK, CHUNK), 1)
    causal = col <= row
    tn = (((0,), (0,)), ((), ()))
    n_steps = NCH * GLA_HEADS
    step = 0

    def chunk_scores(c):
        out = []
        for hd in range(GLA_HEADS):
            rows, cols = slice(c * CHUNK, (c + 1) * CHUNK), slice(hd * GLA_HK, (hd + 1) * GLA_HK)
            sc = lax.dot_general(qt_ref[rows, cols], kt_ref[rows, cols], nt,
                                 preferred_element_type=F32)
            out.append(jnp.where(causal, sc, 0.0).astype(BF16))
        return out

    for _ in range(PIECES_AHEAD):
        pieces.pop(0)()
    scores_next = chunk_scores(0)
    for c in range(NCH):
        r0 = c * CHUNK
        scores_cur = scores_next
        if c + 1 < NCH:
            scores_next = chunk_scores(c + 1)
        anchor = _zero_bits_of(post_norm(c)) if c < TS // LN_ROWS else None
        for hd in range(GLA_HEADS):
            k0 = hd * GLA_HK
            v0 = hd * GLA_HV
            q_c = qt_ref[r0:r0 + CHUNK, k0:k0 + GLA_HK]
            ke_c = kend_ref[r0:r0 + CHUNK, k0:k0 + GLA_HK]
            v_c = v_ref[r0:r0 + CHUNK, v0:v0 + GLA_HV]
            st = state_ref[hd]
            scores = scores_cur[hd]
            lhs = jnp.concatenate([q_c, scores], axis=1)
            rhs = jnp.concatenate([st.astype(BF16), v_c], axis=0)
            og_ref[r0:r0 + CHUNK, v0:v0 + GLA_HV] = jnp.dot(lhs, rhs, preferred_element_type=F32)
            kv = lax.dot_general(ke_c, v_c, tn, preferred_element_type=F32)
            st_new = st * dect_ref[k0:k0 + GLA_HK, c:c + 1] + kv
            state_ref[hd] = st_new
            if anchor is not None and hd == GLA_HEADS - 1:
                state_ref[hd, 0:8, 0:LANES] = _after(st_new[0:8, 0:LANES], anchor)
            step += 1
            while pieces and len(pieces) * n_steps > (n_steps - step) * N_PIECES:
                pieces.pop(0)()
    while pieces:
        pieces.pop(0)()

    o_parts = []
    for hd in range(GLA_HEADS):
        v0 = hd * GLA_HV
        o_h = og_ref[:, v0:v0 + GLA_HV]
        mu = jnp.mean(o_h, axis=-1, keepdims=True)
        dlt = o_h - mu
        var = jnp.mean(dlt * dlt, axis=-1, keepdims=True)
        o_parts.append((dlt * lax.rsqrt(var + LN_EPS) * gng_ref[0, :, v0:v0 + GLA_HV]
                        * rg_ref[:, v0:v0 + GLA_HV]).astype(BF16))
        gg_piece(hd * PIECE)
    o_n = jnp.concatenate(o_parts, axis=-1)
    y_gla = jnp.dot(o_n, wpg_ref[...], preferred_element_type=F32)

    pos = s * TS + lax.broadcasted_iota(jnp.int32, (TS, POOL_GW), 0)
    p_parts = []
    for gi, w in enumerate(POOL_WINDOWS):
        c0 = gi * POOL_GW
        ext = uext_ref[:, c0:c0 + POOL_GW]
        span = 1
        while span < w:
            ext = ext + pltpu.roll(ext, span, axis=0)
            span *= 2
        wsum = ext[POOL_HALO:POOL_HALO + TS]
        u_g = uext_ref[POOL_HALO:POOL_HALO + TS, c0:c0 + POOL_GW]
        cnt = jnp.minimum(pos + 1, w).astype(F32)
        p_g = (wsum / cnt - u_g).astype(BF16)
        p_parts.append(jnp.dot(p_g, wpool_ref[0, gi], preferred_element_type=F32))
    p = (jnp.concatenate(p_parts, axis=-1) * pscale_ref[0]).astype(BF16)
    y_pool = jnp.dot(p, wpp_ref[...], preferred_element_type=F32)

    merged = gg_ref[...] * y_gla + gp_ref[...] * y_pool
    y_ref[...] = jnp.dot(merged.astype(BF16), wout_ref[...], preferred_element_type=F32)


def _mixer_kernel(xp_ref, x_ref, modp_ref, mod_ref, wint_hbm, wpg_hbm, wpp_hbm, wout_hbm,
                  walpha_ref, balpha_ref, gng_ref, wpool_ref, pscale_ref, lng_ref, lnb_ref,
                  o_ref,
                  state_ref, uext_ref, qt_ref, kt_ref, kend_ref, v_ref, dect_ref, og_ref,
                  rg_ref, gg_ref, gp_ref,
                  wint_ref, wpg_ref, wpp_ref, wout_ref, y_ref, stage_ref, sems,
                  *, layer, n_tiles, tiles_per_seq):
    i = pl.program_id(0)

    def post_norm(rb):
        rows = pl.ds(rb * LN_ROWS, LN_ROWS)
        gt = modp_ref[0, 0, 5:6, :]
        t = ALPHA * xp_ref[0, rows, :] + (1.0 + gt) * y_ref[rows, :]
        out = _layer_norm(t, lng_ref[0, 1:2, :], lnb_ref[0, 1:2, :])
        o_ref[0, rows, :] = out
        return out

    @pl.when(i == 0)
    def _():
        y_ref[...] = jnp.zeros_like(y_ref)
        _stage_bf16(
            _row_windows(wint_hbm.at[layer], wint_ref, IN_WIDTH, MIX_STAGE_ROWS)
            + _row_windows(wpg_hbm.at[layer], wpg_ref, GLA_DV, MIX_STAGE_ROWS)
            + _row_windows(wout_hbm.at[layer], wout_ref, D_MODEL, MIX_STAGE_ROWS)
            + _row_windows(wpp_hbm.at[layer], wpp_ref, POOL_WIDTH, MIX_STAGE_ROWS),
            stage_ref, sems)

    @pl.when(i < n_tiles)
    def _():
        _mixer_tile(i % tiles_per_seq, post_norm, x_ref, mod_ref, walpha_ref, balpha_ref, gng_ref,
                    wpool_ref, pscale_ref, state_ref, uext_ref, qt_ref, kt_ref, kend_ref, v_ref,
                    dect_ref, og_ref, rg_ref, gg_ref, gp_ref, wint_ref, wpg_ref, wpp_ref, wout_ref,
                    y_ref)

    @pl.when(i == n_tiles)
    def _():
        for rb in range(MIX_TS // LN_ROWS):
            post_norm(rb)


def _mixer_call(x, mod, big_weights, small_operands, *, layer):
    B, S, D = x.shape
    TS = MIX_TS
    tiles_per_seq = S // TS
    n_tiles = B * tiles_per_seq

    def cur(i):
        return jnp.minimum(i, n_tiles - 1)

    def prev(i):
        return jnp.maximum(i - 1, 0)

    def x_map(tile):
        return lambda i: (tile(i) // tiles_per_seq, tile(i) % tiles_per_seq, 0)

    def mod_map(tile):
        return lambda i: (layer, tile(i) // tiles_per_seq, 0, 0)

    return pl.pallas_call(
        functools.partial(_mixer_kernel, layer=layer, n_tiles=n_tiles, tiles_per_seq=tiles_per_seq),
        grid=(n_tiles + 1,),
        in_specs=[
            pl.BlockSpec((1, TS, D), x_map(prev)),
            pl.BlockSpec((1, TS, D), x_map(cur)),
            pl.BlockSpec((1, 1, N_MOD, D), mod_map(prev)),
            pl.BlockSpec((1, 1, N_MOD, D), mod_map(cur)),
        ] + [pl.BlockSpec(memory_space=pl.ANY) for _ in big_weights]
          + [_layer_spec(a, layer) for a in small_operands],
        out_specs=pl.BlockSpec((1, TS, D), x_map(prev)),
        out_shape=jax.ShapeDtypeStruct((B, S, D), F32),
        scratch_shapes=[
            pltpu.VMEM((GLA_HEADS, GLA_HK, GLA_HV), F32),
            pltpu.VMEM((POOL_HALO + TS, POOL_WIDTH), F32),
            pltpu.VMEM((TS, GLA_DK), BF16),
            pltpu.VMEM((TS, GLA_DK), BF16),
            pltpu.VMEM((TS, GLA_DK), BF16),
            pltpu.VMEM((TS, GLA_DV), BF16),
            pltpu.VMEM((GLA_DK, LANES), F32),
            pltpu.VMEM((TS, GLA_DV), F32),
            pltpu.VMEM((TS, GLA_DV), F32),
            pltpu.VMEM((TS, D_MODEL), F32),
            pltpu.VMEM((TS, D_MODEL), F32),
            pltpu.VMEM((IN_WIDTH, D_MODEL), BF16),
            pltpu.VMEM((GLA_DV, D_MODEL), BF16),
            pltpu.VMEM((POOL_WIDTH, D_MODEL), BF16),
            pltpu.VMEM((D_MODEL, D_MODEL), BF16),
            pltpu.VMEM((TS, D_MODEL), F32),
            pltpu.VMEM((2, MIX_STAGE_ROWS, D_MODEL), F32),
            pltpu.SemaphoreType.DMA((2,)),
        ],
        compiler_params=pltpu.CompilerParams(
            dimension_semantics=("arbitrary",),
            vmem_limit_bytes=VMEM_LIMIT),
        name="mixer",
    )(x, x, mod, mod, *big_weights, *small_operands)


def _mixer_slab_operands(w_alpha, b_alpha, gla_norm_g, w_pool, pool_scale, ln_g, ln_b):
    L = w_alpha.shape[0]
    w_al = jnp.pad(w_alpha, ((0, 0), (0, LANES - GLA_RANK), (0, 0))).astype(BF16)
    return (w_al, b_alpha.reshape(L, 1, GLA_DK), gla_norm_g.reshape(L, 1, GLA_DV),
            w_pool.astype(BF16), pool_scale.reshape(L, 1, POOL_WIDTH), ln_g, ln_b)


def kernel(x, c, w_ada, b_ada, ffn1_up, ffn1_down, w_in, w_alpha, b_alpha, gla_norm_g,
           w_proj_gla, w_pool, pool_scale, w_proj_pool, w_out, ffn2_up, ffn2_down, ln_g, ln_b):
    B, S, D = x.shape
    L = w_ada.shape[0]
    c_pad = jnp.pad(c, ((0, 8 - B), (0, 0)))
    mod = _ada_call(c_pad, w_ada, b_ada)[:, :B].reshape(L, B, N_MOD, D)
    mix_big = (jnp.swapaxes(w_in, 1, 2), w_proj_gla, w_proj_pool, w_out)
    mix_small = _mixer_slab_operands(w_alpha, b_alpha, gla_norm_g, w_pool, pool_scale, ln_g, ln_b)
    x2 = x.reshape(B * S, D)
    for l in range(L):
        x2 = _ffn_call(x2, mod, ffn1_up, ffn1_down, ln_g, ln_b, layer=l, mod_base=0, ln_row=0,
                       seq=S)
        x3 = _mixer_call(x2.reshape(B, S, D), mod, mix_big, mix_small, layer=l)
        x2 = _ffn_call(x3.reshape(B * S, D), mod, ffn2_up, ffn2_down, ln_g, ln_b, layer=l,
                       mod_base=6, ln_row=2, seq=S)
    return x2.reshape(B, S, D)
```

```python
import functools

import jax
import jax.numpy as jnp
from jax import lax
from jax.experimental import pallas as pl
from jax.experimental.pallas import tpu as pltpu

F32 = jnp.float32
BF16 = jnp.bfloat16

D_MODEL = 1024
DEPTH = 4
GLA_HEADS = 4
GLA_DK = 512
GLA_DV = 1024
GLA_HK = 128
GLA_HV = 256
GLA_RANK = 16
GLA_TAU = 16.0
CHUNK = 64
POOL_WIDTH = 512
POOL_WINDOWS = (2, 4, 8, 16)
POOL_GW = 128
POOL_HALO = 16
D_FF = 2816
N_MOD = 9
ALPHA = (2 * DEPTH) ** 0.25
LN_EPS = 1e-5
LANES = 128

FFN_TM = 512
FFN_TF = 256
MIX_TS = 512
LN_ROWS = 64
RING_ROWS = 512
RING_SLOTS = 3
CUM_ROWS = 256
ADA_TN = 4608
VMEM_LIMIT = 56 * 1024 * 1024

IN_Q = 0
IN_K = IN_Q + GLA_DK
IN_V = IN_K + GLA_DK
IN_R = IN_V + GLA_DV
IN_A = IN_R + GLA_DV
IN_U = IN_A + GLA_RANK
IN_GG = IN_U + POOL_WIDTH
IN_GP = IN_GG + D_MODEL
IN_WIDTH = IN_GP + D_MODEL
MIX_IN_SEGMENTS = (("au", IN_A, IN_GG - IN_A), ("v", IN_V, GLA_DV), ("qk", IN_Q, 2 * GLA_DK),
                   ("r", IN_R, GLA_DV), ("gp", IN_GP, D_MODEL), ("gg", IN_GG, D_MODEL))
PIECE = 256
PIECES_AHEAD = 2
N_PIECES = (GLA_DV + POOL_WIDTH + D_MODEL) // PIECE
CUM_TERMS = 2


def _layer_norm(t, g, b):
    mu = jnp.mean(t, axis=-1, keepdims=True)
    d = t - mu
    var = jnp.mean(d * d, axis=-1, keepdims=True)
    return d * lax.rsqrt(var + LN_EPS) * g + b


def _silu(t):
    return t * jax.nn.sigmoid(t)


def _zero_bits_of(v):
    bits = lax.bitcast_convert_type(v, jnp.uint32)
    acc = None
    for r0 in range(0, v.shape[0], 8):
        for c0 in range(0, v.shape[1], LANES):
            tile = bits[r0:r0 + 8, c0:c0 + LANES]
            acc = tile if acc is None else acc | tile
    return lax.shift_right_logical(lax.shift_right_logical(acc, jnp.uint32(16)), jnp.uint32(16))


def _after(v, zero_bits):
    reps = v.shape[0] // zero_bits.shape[0]
    z = jnp.concatenate([zero_bits] * reps, axis=0) if reps > 1 else zero_bits
    return jnp.where(z == 0, v, 0.0)


def _window_stream(windows, stage_ref, sems):
    n_slots = stage_ref.shape[0]

    def copy(j):
        src, _ = windows[j]
        slot = stage_ref.at[j % n_slots, 0:src.shape[0], 0:src.shape[1]]
        return pltpu.make_async_copy(src, slot, sems.at[j % n_slots])

    def start(j):
        if j < len(windows):
            copy(j).start()

    def wait(j):
        copy(j).wait()

    def convert(j):
        src, dst = windows[j]
        dst[...] = stage_ref[j % n_slots, 0:src.shape[0], 0:src.shape[1]].astype(BF16)

    return start, wait, convert


def _row_windows(src, dst, n_rows, step):
    return [(src.at[pl.ds(r0, min(step, n_rows - r0)), :], dst.at[pl.ds(r0, min(step, n_rows - r0)), :])
            for r0 in range(0, n_rows, step)]


def _layer_spec(a, layer):
    nd = a.ndim
    return pl.BlockSpec((1,) + a.shape[1:], lambda *_: (layer,) + (0,) * (nd - 1),
                        pipeline_mode=pl.Buffered(1))


def _ada_kernel(c_ref, w_ref, b_ref, o_ref):
    c_act = _silu(c_ref[...]).astype(BF16)
    w = w_ref[0].astype(BF16)
    o_ref[0] = jnp.dot(c_act, w, preferred_element_type=F32) + b_ref[0]


def _ada_call(c_pad, w_ada, b_ada):
    L, D, N = w_ada.shape
    rows = c_pad.shape[0]
    return pl.pallas_call(
        _ada_kernel,
        grid=(L, N // ADA_TN),
        in_specs=[
            pl.BlockSpec((rows, D), lambda l, n: (0, 0)),
            pl.BlockSpec((1, D, ADA_TN), lambda l, n: (l, 0, n)),
            pl.BlockSpec((1, 1, ADA_TN), lambda l, n: (l, 0, n)),
        ],
        out_specs=pl.BlockSpec((1, rows, ADA_TN), lambda l, n: (l, 0, n)),
        out_shape=jax.ShapeDtypeStruct((L, rows, N), F32),
        compiler_params=pltpu.CompilerParams(
            dimension_semantics=("arbitrary", "arbitrary"),
            vmem_limit_bytes=VMEM_LIMIT),
        name="ada_mod",
    )(c_pad, w_ada, b_ada.reshape(L, 1, N))


def _ffn_kernel(xp_ref, x_ref, modp_ref, mod_ref, wup_hbm, wdn_hbm, lng_ref, lnb_ref,
                o_ref, a_ref, y_ref, wup_ref, wdn_ref, gstage_ref, ustage_ref, dstage_ref,
                g_sems, u_sems, d_sems, *, layer, mod_base, ln_row, n_tiles):
    i = pl.program_id(0)
    chunks = range(0, D_FF, FFN_TF)
    streams = [
        _window_stream([(wup_hbm.at[layer, :, pl.ds(c0, FFN_TF)], wup_ref.at[:, pl.ds(c0, FFN_TF)])
                        for c0 in chunks], gstage_ref, g_sems),
        _window_stream([(wup_hbm.at[layer, :, pl.ds(D_FF + c0, FFN_TF)],
                         wup_ref.at[:, pl.ds(D_FF + c0, FFN_TF)]) for c0 in chunks],
                       ustage_ref, u_sems),
        _window_stream(_row_windows(wdn_hbm.at[layer], wdn_ref, D_FF, FFN_TF), dstage_ref, d_sems),
    ]

    def post_norm(rb):
        rows = pl.ds(rb * LN_ROWS, LN_ROWS)
        gt = modp_ref[0, 0, mod_base + 2:mod_base + 3, :]
        t = ALPHA * xp_ref[rows, :] + (0.5 * (1.0 + gt)) * y_ref[rows, :]
        out = _layer_norm(t, lng_ref[0, ln_row:ln_row + 1, :], lnb_ref[0, ln_row:ln_row + 1, :])
        o_ref[rows, :] = out
        return out

    def tile(stage_weights):
        if stage_weights:
            for start, _, _ in streams:
                start(0)
                start(1)
        sh = mod_ref[0, 0, mod_base:mod_base + 1, :]
        sc = mod_ref[0, 0, mod_base + 1:mod_base + 2, :]
        h = (x_ref[...] * (1.0 + sc) + sh).astype(BF16)
        anchor = None
        for ci, c0 in enumerate(chunks):
            if stage_weights:
                for _, wait, _ in streams:
                    wait(ci)
                for start, _, convert in streams:
                    convert(ci)
                    start(ci + 2)
            g = jnp.dot(h, wup_ref[:, c0:c0 + FFN_TF], preferred_element_type=F32)
            u = jnp.dot(h, wup_ref[:, D_FF + c0:D_FF + c0 + FFN_TF], preferred_element_type=F32)
            act = _silu(g) * u
            a_ref[:, c0:c0 + FFN_TF] = act.astype(BF16)
            if anchor is not None:
                a_ref[0:16, c0:c0 + LANES] = _after(act[0:16, 0:LANES], anchor).astype(BF16)
            anchor = _zero_bits_of(post_norm(ci)) if ci < FFN_TM // LN_ROWS else None
        y_ref[...] = jnp.dot(a_ref[...], wdn_ref[...], preferred_element_type=F32)

    @pl.when(i == 0)
    def _():
        y_ref[...] = jnp.zeros_like(y_ref)
        tile(stage_weights=True)

    @pl.when((i > 0) & (i < n_tiles))
    def _():
        tile(stage_weights=False)

    @pl.when(i == n_tiles)
    def _():
        for rb in range(FFN_TM // LN_ROWS):
            post_norm(rb)


def _ffn_call(x2, mod, w_up, w_down, ln_g, ln_b, *, layer, mod_base, ln_row, seq):
    M, D = x2.shape
    n_tiles = M // FFN_TM
    tiles_per_seq = seq // FFN_TM
    assert D_FF // FFN_TF >= FFN_TM // LN_ROWS

    def cur(i):
        return jnp.minimum(i, n_tiles - 1)

    def prev(i):
        return jnp.maximum(i - 1, 0)

    return pl.pallas_call(
        functools.partial(_ffn_kernel, layer=layer, mod_base=mod_base, ln_row=ln_row,
                          n_tiles=n_tiles),
        grid=(n_tiles + 1,),
        in_specs=[
            pl.BlockSpec((FFN_TM, D), lambda i: (prev(i), 0)),
            pl.BlockSpec((FFN_TM, D), lambda i: (cur(i), 0)),
            pl.BlockSpec((1, 1, N_MOD, D), lambda i: (layer, prev(i) // tiles_per_seq, 0, 0)),
            pl.BlockSpec((1, 1, N_MOD, D), lambda i: (layer, cur(i) // tiles_per_seq, 0, 0)),
            pl.BlockSpec(memory_space=pl.ANY),
            pl.BlockSpec(memory_space=pl.ANY),
            _layer_spec(ln_g, layer),
            _layer_spec(ln_b, layer),
        ],
        out_specs=pl.BlockSpec((FFN_TM, D), lambda i: (prev(i), 0)),
        out_shape=jax.ShapeDtypeStruct((M, D), F32),
        scratch_shapes=[
            pltpu.VMEM((FFN_TM, D_FF), BF16),
            pltpu.VMEM((FFN_TM, D), F32),
            pltpu.VMEM((D, 2 * D_FF), BF16),
            pltpu.VMEM((D_FF, D), BF16),
            pltpu.VMEM((2, D, FFN_TF), F32),
            pltpu.VMEM((2, D, FFN_TF), F32),
            pltpu.VMEM((2, FFN_TF, D), F32),
            pltpu.SemaphoreType.DMA((2,)),
            pltpu.SemaphoreType.DMA((2,)),
            pltpu.SemaphoreType.DMA((2,)),
        ],
        compiler_params=pltpu.CompilerParams(
            dimension_semantics=("arbitrary",),
            vmem_limit_bytes=VMEM_LIMIT),
        name="ffn",
    )(x2, x2, mod, mod, w_up, w_down, ln_g, ln_b)


def _split_bf16(t, parts):
    out = []
    r = t
    for _ in range(parts):
        p = r.astype(BF16)
        out.append(p)
        r = r - p.astype(F32)
    return out


def _mixer_stager(layer, wint_hbm, wpg_hbm, wpp_hbm, wout_hbm, wint_ref, wpg_ref, wpp_ref, wout_ref,
                  stage_ref, sems):
    segments = ([(name, wint_hbm, wint_ref, row0, rows) for name, row0, rows in MIX_IN_SEGMENTS]
                + [("pg", wpg_hbm, wpg_ref, 0, GLA_DV), ("pp", wpp_hbm, wpp_ref, 0, POOL_WIDTH),
                   ("out", wout_hbm, wout_ref, 0, D_MODEL)])
    windows, first = [], {}
    for name, src, dst, row0, rows in segments:
        first[name] = (len(windows), row0)
        for r in range(row0, row0 + rows, RING_ROWS):
            n = min(RING_ROWS, row0 + rows - r)
            windows.append((src.at[layer, pl.ds(r, n), :], dst.at[pl.ds(r, n), :]))
    start, wait, convert = _window_stream(windows, stage_ref, sems)
    n_slots = stage_ref.shape[0]
    done = [0]

    def begin():
        for j in range(n_slots):
            start(j)

    def need(name, row_end):
        index0, row0 = first[name]
        upto = index0 + -(-(row_end - row0) // RING_ROWS)
        for j in range(done[0], upto):
            wait(j)
            convert(j)
            start(j + n_slots)
        done[0] = max(done[0], upto)

    return begin, need


def _mixer_tile(s, post_norm, x_ref, mod_ref, walpha_ref, balpha_ref, gng_ref, wpool_ref, pscale_ref,
                state_ref, uext_ref, qt_ref, kt_ref, kend_ref, v_ref, dect_ref, og_ref,
                rg_ref, gg_ref, gp_ref, wint_ref, wpg_ref, wpp_ref, wout_ref, y_ref):
    TS = MIX_TS
    NCH = TS // CHUNK

    @pl.when(s == 0)
    def _():
        state_ref[...] = jnp.zeros_like(state_ref)
        uext_ref[0:POOL_HALO, :] = jnp.zeros((POOL_HALO, POOL_WIDTH), F32)

    @pl.when(s > 0)
    def _():
        uext_ref[0:POOL_HALO, :] = uext_ref[TS:TS + POOL_HALO, :]

    x = x_ref[0]
    sh = mod_ref[0, 0, 3:4, :]
    sc = mod_ref[0, 0, 4:5, :]
    h = (x * (1.0 + sc) + sh).astype(BF16)

    nt = (((1,), (1,)), ((), ()))

    def proj(r0, r1):
        return lax.dot_general(h, wint_ref[r0:r1, :], nt, preferred_element_type=F32)

    a_wide = proj(IN_A, IN_A + LANES)
    v_ref[:, 0:PIECE] = proj(IN_V, IN_V + PIECE).astype(BF16)
    lane = lax.broadcasted_iota(jnp.int32, a_wide.shape, 1)
    a_lr = jnp.where(lane < GLA_RANK, a_wide, 0.0).astype(BF16)
    pre = jnp.dot(a_lr, walpha_ref[0], preferred_element_type=F32) + balpha_ref[0]
    log_a = (jnp.minimum(pre, 0.0) - jnp.log(1.0 + jnp.exp(-jnp.abs(pre)))) / GLA_TAU

    v_ref[:, PIECE:GLA_DV] = proj(IN_V + PIECE, IN_R).astype(BF16)

    ri = lax.broadcasted_iota(jnp.int32, (CUM_ROWS, CUM_ROWS), 0)
    ci = lax.broadcasted_iota(jnp.int32, (CUM_ROWS, CUM_ROWS), 1)
    tri = jnp.where((ri // CHUNK == ci // CHUNK) & (ci <= ri), 1.0, 0.0).astype(BF16)
    b_parts = []
    for g0 in range(0, TS, CUM_ROWS):
        acc = None
        for part in _split_bf16(log_a[g0:g0 + CUM_ROWS], CUM_TERMS):
            t = jnp.dot(tri, part, preferred_element_type=F32)
            acc = t if acc is None else acc + t
        b_parts.append(acc)
    b = jnp.concatenate(b_parts, axis=0)

    q = proj(IN_Q, IN_K) * (GLA_HK ** -0.5)
    k = proj(IN_K, IN_V)
    qt_ref[...] = (q * jnp.exp(b)).astype(BF16)
    kt_ref[...] = (k * jnp.exp(-b)).astype(BF16)
    dec_rows = []
    for c in range(NCH):
        r0 = c * CHUNK
        b_c = b[r0:r0 + CHUNK]
        b_last = b_c[CHUNK - 1:CHUNK, :]
        kend_ref[r0:r0 + CHUNK, :] = (k[r0:r0 + CHUNK] * jnp.exp(b_last - b_c)).astype(BF16)
        dec_rows.append(jnp.exp(b_last))
    dec_all = jnp.concatenate(
        dec_rows + [jnp.zeros((LANES - NCH, GLA_DK), F32)], axis=0)
    dect_ref[...] = dec_all.T

    def r_piece(c0):
        rg_ref[:, c0:c0 + PIECE] = _silu(proj(IN_R + c0, IN_R + c0 + PIECE))

    def u_piece(c0):
        uext_ref[POOL_HALO:POOL_HALO + TS, c0:c0 + PIECE] = proj(IN_U + c0, IN_U + c0 + PIECE)

    def gp_piece(c0):
        gp_ref[:, c0:c0 + PIECE] = jax.nn.sigmoid(proj(IN_GP + c0, IN_GP + c0 + PIECE))

    def gg_piece(c0):
        gg_ref[:, c0:c0 + PIECE] = jax.nn.sigmoid(proj(IN_GG + c0, IN_GG + c0 + PIECE))

    pieces = ([functools.partial(r_piece, c0) for c0 in range(0, GLA_DV, PIECE)]
              + [functools.partial(u_piece, c0) for c0 in range(0, POOL_WIDTH, PIECE)]
              + [functools.partial(gp_piece, c0) for c0 in range(0, D_MODEL, PIECE)])

    row = lax.broadcasted_iota(jnp.int32, (CHUNK, CHUNK), 0)
    col = lax.broadcasted_iota(jnp.int32, (CHUNK, CHUNK), 1)
    causal = col <= row
    tn = (((0,), (0,)), ((), ()))
    n_steps = NCH * GLA_HEADS
    step = 0

    def chunk_scores(c):
        out = []
        for hd in range(GLA_HEADS):
            rows, cols = slice(c * CHUNK, (c + 1) * CHUNK), slice(hd * GLA_HK, (hd + 1) * GLA_HK)
            sc = lax.dot_general(qt_ref[rows, cols], kt_ref[rows, cols], nt,
                                 preferred_element_type=F32)
            out.append(jnp.where(causal, sc, 0.0).astype(BF16))
        return out

    for _ in range(PIECES_AHEAD):
        pieces.pop(0)()
    scores_next = chunk_scores(0)
    for c in range(NCH):
        r0 = c * CHUNK
        scores_cur = scores_next
        if c + 1 < NCH:
            scores_next = chunk_scores(c + 1)
        anchor = _zero_bits_of(post_norm(c)) if c < TS // LN_ROWS else None
        for hd in range(GLA_HEADS):
            k0 = hd * GLA_HK
            v0 = hd * GLA_HV
            q_c = qt_ref[r0:r0 + CHUNK, k0:k0 + GLA_HK]
            ke_c = kend_ref[r0:r0 + CHUNK, k0:k0 + GLA_HK]
            v_c = v_ref[r0:r0 + CHUNK, v0:v0 + GLA_HV]
            st = state_ref[hd]
            scores = scores_cur[hd]
            lhs = jnp.concatenate([q_c, scores], axis=1)
            rhs = jnp.concatenate([st.astype(BF16), v_c], axis=0)
            og_ref[r0:r0 + CHUNK, v0:v0 + GLA_HV] = jnp.dot(lhs, rhs, preferred_element_type=F32)
            kv = lax.dot_general(ke_c, v_c, tn, preferred_element_type=F32)
            st_new = st * dect_ref[k0:k0 + GLA_HK, c:c + 1] + kv
            state_ref[hd] = st_new
            if anchor is not None and hd == GLA_HEADS - 1:
                state_ref[hd, 0:8, 0:LANES] = _after(st_new[0:8, 0:LANES], anchor)
            step += 1
            while pieces and len(pieces) * n_steps > (n_steps - step) * N_PIECES:
                pieces.pop(0)()
    while pieces:
        pieces.pop(0)()

    o_parts = []
    for hd in range(GLA_HEADS):
        v0 = hd * GLA_HV
        o_h = og_ref[:, v0:v0 + GLA_HV]
        mu = jnp.mean(o_h, axis=-1, keepdims=True)
        dlt = o_h - mu
        var = jnp.mean(dlt * dlt, axis=-1, keepdims=True)
        o_parts.append((dlt * lax.rsqrt(var + LN_EPS) * gng_ref[0, :, v0:v0 + GLA_HV]
                        * rg_ref[:, v0:v0 + GLA_HV]).astype(BF16))
        gg_piece(hd * PIECE)
    o_n = jnp.concatenate(o_parts, axis=-1)
    y_gla = jnp.dot(o_n, wpg_ref[...], preferred_element_type=F32)

    pos = s * TS + lax.broadcasted_iota(jnp.int32, (TS, POOL_GW), 0)
    p_parts = []
    for gi, w in enumerate(POOL_WINDOWS):
        c0 = gi * POOL_GW
        ext = uext_ref[:, c0:c0 + POOL_GW]
        span = 1
        while span < w:
            ext = ext + pltpu.roll(ext, span, axis=0)
            span *= 2
        wsum = ext[POOL_HALO:POOL_HALO + TS]
        u_g = uext_ref[POOL_HALO:POOL_HALO + TS, c0:c0 + POOL_GW]
        cnt = jnp.minimum(pos + 1, w).astype(F32)
        p_g = (wsum / cnt - u_g).astype(BF16)
        p_parts.append(jnp.dot(p_g, wpool_ref[0, gi], preferred_element_type=F32))
    p = (jnp.concatenate(p_parts, axis=-1) * pscale_ref[0]).astype(BF16)
    y_pool = jnp.dot(p, wpp_ref[...], preferred_element_type=F32)

    merged = gg_ref[...] * y_gla + gp_ref[...] * y_pool
    y_ref[...] = jnp.dot(merged.astype(BF16), wout_ref[...], preferred_element_type=F32)


def _mixer_kernel(xp_ref, x_ref, modp_ref, mod_ref, wint_hbm, wpg_hbm, wpp_hbm, wout_hbm,
                  walpha_ref, balpha_ref, gng_ref, wpool_ref, pscale_ref, lng_ref, lnb_ref,
                  o_ref,
                  state_ref, uext_ref, qt_ref, kt_ref, kend_ref, v_ref, dect_ref, og_ref,
                  rg_ref, gg_ref, gp_ref,
                  wint_ref, wpg_ref, wpp_ref, wout_ref, y_ref, stage_ref, sems,
                  *, layer, n_tiles, tiles_per_seq):
    i = pl.program_id(0)

    def post_norm(rb):
        rows = pl.ds(rb * LN_ROWS, LN_ROWS)
        gt = modp_ref[0, 0, 5:6, :]
        t = ALPHA * xp_ref[0, rows, :] + (1.0 + gt) * y_ref[rows, :]
        out = _layer_norm(t, lng_ref[0, 1:2, :], lnb_ref[0, 1:2, :])
        o_ref[0, rows, :] = out
        return out

    @pl.when(i == 0)
    def _():
        y_ref[...] = jnp.zeros_like(y_ref)
        begin, need = _mixer_stager(layer, wint_hbm, wpg_hbm, wpp_hbm, wout_hbm, wint_ref, wpg_ref,
                                    wpp_ref, wout_ref, stage_ref, sems)
        begin()
        need("out", D_MODEL)

    @pl.when(i < n_tiles)
    def _():
        _mixer_tile(i % tiles_per_seq, post_norm, x_ref, mod_ref, walpha_ref, balpha_ref, gng_ref,
                    wpool_ref, pscale_ref, state_ref, uext_ref, qt_ref, kt_ref, kend_ref, v_ref,
                    dect_ref, og_ref, rg_ref, gg_ref, gp_ref, wint_ref, wpg_ref, wpp_ref, wout_ref,
                    y_ref)

    @pl.when(i == n_tiles)
    def _():
        for rb in range(MIX_TS // LN_ROWS):
            post_norm(rb)


def _mixer_call(x, mod, big_weights, small_operands, *, layer):
    B, S, D = x.shape
    TS = MIX_TS
    tiles_per_seq = S // TS
    n_tiles = B * tiles_per_seq

    def cur(i):
        return jnp.minimum(i, n_tiles - 1)

    def prev(i):
        return jnp.maximum(i - 1, 0)

    def x_map(tile):
        return lambda i: (tile(i) // tiles_per_seq, tile(i) % tiles_per_seq, 0)

    def mod_map(tile):
        return lambda i: (layer, tile(i) // tiles_per_seq, 0, 0)

    return pl.pallas_call(
        functools.partial(_mixer_kernel, layer=layer, n_tiles=n_tiles, tiles_per_seq=tiles_per_seq),
        grid=(n_tiles + 1,),
        in_specs=[
            pl.BlockSpec((1, TS, D), x_map(prev)),
            pl.BlockSpec((1, TS, D), x_map(cur)),
            pl.BlockSpec((1, 1, N_MOD, D), mod_map(prev)),
            pl.BlockSpec((1, 1, N_MOD, D), mod_map(cur)),
        ] + [pl.BlockSpec(memory_space=pl.ANY) for _ in big_weights]
          + [_layer_spec(a, layer) for a in small_operands],
        out_specs=pl.BlockSpec((1, TS, D), x_map(prev)),
        out_shape=jax.ShapeDtypeStruct((B, S, D), F32),
        scratch_shapes=[
            pltpu.VMEM((GLA_HEADS, GLA_HK, GLA_HV), F32),
            pltpu.VMEM((POOL_HALO + TS, POOL_WIDTH), F32),
            pltpu.VMEM((TS, GLA_DK), BF16),
            pltpu.VMEM((TS, GLA_DK), BF16),
            pltpu.VMEM((TS, GLA_DK), BF16),
            pltpu.VMEM((TS, GLA_DV), BF16),
            pltpu.VMEM((GLA_DK, LANES), F32),
            pltpu.VMEM((TS, GLA_DV), F32),
            pltpu.VMEM((TS, GLA_DV), F32),
            pltpu.VMEM((TS, D_MODEL), F32),
            pltpu.VMEM((TS, D_MODEL), F32),
            pltpu.VMEM((IN_WIDTH, D_MODEL), BF16),
            pltpu.VMEM((GLA_DV, D_MODEL), BF16),
            pltpu.VMEM((POOL_WIDTH, D_MODEL), BF16),
            pltpu.VMEM((D_MODEL, D_MODEL), BF16),
            pltpu.VMEM((TS, D_MODEL), F32),
            pltpu.VMEM((RING_SLOTS, RING_ROWS, D_MODEL), F32),
            pltpu.SemaphoreType.DMA((RING_SLOTS,)),
        ],
        compiler_params=pltpu.CompilerParams(
            dimension_semantics=("arbitrary",),
            vmem_limit_bytes=VMEM_LIMIT),
        name="mixer",
    )(x, x, mod, mod, *big_weights, *small_operands)


def _mixer_slab_operands(w_alpha, b_alpha, gla_norm_g, w_pool, pool_scale, ln_g, ln_b):
    L = w_alpha.shape[0]
    w_al = jnp.pad(w_alpha, ((0, 0), (0, LANES - GLA_RANK), (0, 0))).astype(BF16)
    return (w_al, b_alpha.reshape(L, 1, GLA_DK), gla_norm_g.reshape(L, 1, GLA_DV),
            w_pool.astype(BF16), pool_scale.reshape(L, 1, POOL_WIDTH), ln_g, ln_b)


def kernel(x, c, w_ada, b_ada, ffn1_up, ffn1_down, w_in, w_alpha, b_alpha, gla_norm_g,
           w_proj_gla, w_pool, pool_scale, w_proj_pool, w_out, ffn2_up, ffn2_down, ln_g, ln_b):
    B, S, D = x.shape
    L = w_ada.shape[0]
    c_pad = jnp.pad(c, ((0, 8 - B), (0, 0)))
    mod = _ada_call(c_pad, w_ada, b_ada)[:, :B].reshape(L, B, N_MOD, D)
    mix_big = (jnp.swapaxes(w_in, 1, 2), w_proj_gla, w_proj_pool, w_out)
    mix_small = _mixer_slab_operands(w_alpha, b_alpha, gla_norm_g, w_pool, pool_scale, ln_g, ln_b)
    x2 = x.reshape(B * S, D)
    for l in range(L):
        x2 = _ffn_call(x2, mod, ffn1_up, ffn1_down, ln_g, ln_b, layer=l, mod_base=0, ln_row=0,
                       seq=S)
        x3 = _mixer_call(x2.reshape(B, S, D), mod, mix_big, mix_small, layer=l)
        x2 = _ffn_call(x3.reshape(B * S, D), mod, ffn2_up, ffn2_down, ln_g, ln_b, layer=l,
                       mod_base=6, ln_row=2, seq=S)
    return x2.reshape(B, S, D)
```

```python
import functools

import jax
import jax.numpy as jnp
from jax import lax
from jax.experimental import pallas as pl
from jax.experimental.pallas import tpu as pltpu

F32 = jnp.float32
BF16 = jnp.bfloat16

D_MODEL = 1024
DEPTH = 4
GLA_HEADS = 4
GLA_DK = 512
GLA_DV = 1024
GLA_HK = 128
GLA_HV = 256
GLA_RANK = 16
GLA_TAU = 16.0
CHUNK = 64
POOL_WIDTH = 512
POOL_WINDOWS = (2, 4, 8, 16)
POOL_GW = 128
POOL_HALO = 16
D_FF = 2816
N_MOD = 9
ALPHA = (2 * DEPTH) ** 0.25
LN_EPS = 1e-5
LANES = 128
SUBLANES = 8
BF16_ROWS = 16
V7X_VMEM_BYTES = 64 * 1024 * 1024

FFN_TM = 512
FFN_TF = 256
MIX_TS = 512
LN_ROWS = 64
RING_ROWS = 512
RING_SLOTS = 3
CUM_ROWS = 256
ADA_TN = 2304
VMEM_LIMIT = V7X_VMEM_BYTES * 7 // 8

IN_Q = 0
IN_K = IN_Q + GLA_DK
IN_V = IN_K + GLA_DK
IN_R = IN_V + GLA_DV
IN_A = IN_R + GLA_DV
IN_U = IN_A + GLA_RANK
IN_GG = IN_U + POOL_WIDTH
IN_GP = IN_GG + D_MODEL
IN_WIDTH = IN_GP + D_MODEL
PIECE = 256
PIECES_AHEAD = 2
N_PIECES = (GLA_DV + POOL_WIDTH + D_MODEL) // PIECE
CUM_TERMS = 2


def _layer_norm(t, g, b):
    mu = jnp.mean(t, axis=-1, keepdims=True)
    d = t - mu
    var = jnp.mean(d * d, axis=-1, keepdims=True)
    return d * lax.rsqrt(var + LN_EPS) * g + b


def _silu(t):
    return t * jax.nn.sigmoid(t)


def _zero_bits_of(v):
    bits = lax.bitcast_convert_type(v, jnp.uint32)
    acc = None
    for r0 in range(0, v.shape[0], SUBLANES):
        for c0 in range(0, v.shape[1], LANES):
            tile = bits[r0:r0 + SUBLANES, c0:c0 + LANES]
            acc = tile if acc is None else acc | tile
    return lax.shift_right_logical(lax.shift_right_logical(acc, jnp.uint32(16)), jnp.uint32(16))


def _after(v, zero_bits):
    reps = v.shape[0] // zero_bits.shape[0]
    z = jnp.concatenate([zero_bits] * reps, axis=0) if reps > 1 else zero_bits
    return jnp.where(z == 0, v, 0.0)


def _window_stream(windows, stage_ref, sems):
    n_slots = stage_ref.shape[0]

    def copy(j):
        src, _ = windows[j]
        slot = stage_ref.at[j % n_slots, 0:src.shape[0], 0:src.shape[1]]
        return pltpu.make_async_copy(src, slot, sems.at[j % n_slots])

    def start(j):
        if j < len(windows):
            copy(j).start()

    def wait(j):
        copy(j).wait()

    def convert(j):
        src, dst = windows[j]
        dst[...] = stage_ref[j % n_slots, 0:src.shape[0], 0:src.shape[1]].astype(BF16)

    return start, wait, convert


def _stage_bf16(windows, stage_ref, sems):
    start, wait, convert = _window_stream(windows, stage_ref, sems)
    n_slots = stage_ref.shape[0]
    for j in range(n_slots):
        start(j)
    for j in range(len(windows)):
        wait(j)
        convert(j)
        start(j + n_slots)


def _row_windows(src, dst, n_rows, step):
    return [(src.at[pl.ds(r0, min(step, n_rows - r0)), :], dst.at[pl.ds(r0, min(step, n_rows - r0)), :])
            for r0 in range(0, n_rows, step)]


def _layer_spec(a, layer):
    nd = a.ndim
    return pl.BlockSpec((1,) + a.shape[1:], lambda *_: (layer,) + (0,) * (nd - 1),
                        pipeline_mode=pl.Buffered(1))


def _ada_kernel(c_ref, w_ref, b_ref, o_ref):
    c_act = _silu(c_ref[...]).astype(BF16)
    w = w_ref[0].astype(BF16)
    o_ref[0] = jnp.dot(c_act, w, preferred_element_type=F32) + b_ref[0]


def _ada_call(c_pad, w_ada, b_ada):
    L, D, N = w_ada.shape
    rows = c_pad.shape[0]
    return pl.pallas_call(
        _ada_kernel,
        grid=(L, N // ADA_TN),
        in_specs=[
            pl.BlockSpec((rows, D), lambda l, n: (0, 0)),
            pl.BlockSpec((1, D, ADA_TN), lambda l, n: (l, 0, n)),
            pl.BlockSpec((1, 1, ADA_TN), lambda l, n: (l, 0, n)),
        ],
        out_specs=pl.BlockSpec((1, rows, ADA_TN), lambda l, n: (l, 0, n)),
        out_shape=jax.ShapeDtypeStruct((L, rows, N), F32),
        compiler_params=pltpu.CompilerParams(
            dimension_semantics=("arbitrary", "arbitrary"),
            vmem_limit_bytes=VMEM_LIMIT),
        name="ada_mod",
    )(c_pad, w_ada, b_ada.reshape(L, 1, N))


def _ffn_kernel(xp_ref, x_ref, modp_ref, mod_ref, wup_hbm, wdn_hbm, lng_ref, lnb_ref,
                o_ref, a_ref, y_ref, wup_ref, wdn_ref, gstage_ref, ustage_ref, dstage_ref,
                g_sems, u_sems, d_sems, *, layer, mod_base, ln_row, n_tiles):
    i = pl.program_id(0)
    chunks = range(0, D_FF, FFN_TF)
    streams = [
        _window_stream([(wup_hbm.at[layer, :, pl.ds(c0, FFN_TF)], wup_ref.at[:, pl.ds(c0, FFN_TF)])
                        for c0 in chunks], gstage_ref, g_sems),
        _window_stream([(wup_hbm.at[layer, :, pl.ds(D_FF + c0, FFN_TF)],
                         wup_ref.at[:, pl.ds(D_FF + c0, FFN_TF)]) for c0 in chunks],
                       ustage_ref, u_sems),
        _window_stream(_row_windows(wdn_hbm.at[layer], wdn_ref, D_FF, FFN_TF), dstage_ref, d_sems),
    ]

    def post_norm(rb):
        rows = pl.ds(rb * LN_ROWS, LN_ROWS)
        gt = modp_ref[0, 0, mod_base + 2:mod_base + 3, :]
        t = ALPHA * xp_ref[rows, :] + (0.5 * (1.0 + gt)) * y_ref[rows, :]
        out = _layer_norm(t, lng_ref[0, ln_row:ln_row + 1, :], lnb_ref[0, ln_row:ln_row + 1, :])
        o_ref[rows, :] = out
        return out

    def tile(stage_weights):
        if stage_weights:
            for start, _, _ in streams:
                start(0)
                start(1)
        sh = mod_ref[0, 0, mod_base:mod_base + 1, :]
        sc = mod_ref[0, 0, mod_base + 1:mod_base + 2, :]
        h = (x_ref[...] * (1.0 + sc) + sh).astype(BF16)
        anchor = None
        for ci, c0 in enumerate(chunks):
            if stage_weights:
                for _, wait, _ in streams:
                    wait(ci)
                for start, _, convert in streams:
                    convert(ci)
                    start(ci + 2)
            g = jnp.dot(h, wup_ref[:, c0:c0 + FFN_TF], preferred_element_type=F32)
            u = jnp.dot(h, wup_ref[:, D_FF + c0:D_FF + c0 + FFN_TF], preferred_element_type=F32)
            act = _silu(g) * u
            a_ref[:, c0:c0 + FFN_TF] = act.astype(BF16)
            if anchor is not None:
                a_ref[0:BF16_ROWS, c0:c0 + LANES] = _after(act[0:BF16_ROWS, 0:LANES],
                                                           anchor).astype(BF16)
            anchor = _zero_bits_of(post_norm(ci)) if ci < FFN_TM // LN_ROWS else None
        y_ref[...] = jnp.dot(a_ref[...], wdn_ref[...], preferred_element_type=F32)

    @pl.when(i == 0)
    def _():
        y_ref[...] = jnp.zeros_like(y_ref)
        tile(stage_weights=True)

    @pl.when((i > 0) & (i < n_tiles))
    def _():
        tile(stage_weights=False)

    @pl.when(i == n_tiles)
    def _():
        for rb in range(FFN_TM // LN_ROWS):
            post_norm(rb)


def _ffn_call(x2, mod, w_up, w_down, ln_g, ln_b, *, layer, mod_base, ln_row, seq):
    M, D = x2.shape
    n_tiles = M // FFN_TM
    tiles_per_seq = seq // FFN_TM
    assert D_FF // FFN_TF >= FFN_TM // LN_ROWS

    def cur(i):
        return jnp.minimum(i, n_tiles - 1)

    def prev(i):
        return jnp.maximum(i - 1, 0)

    return pl.pallas_call(
        functools.partial(_ffn_kernel, layer=layer, mod_base=mod_base, ln_row=ln_row,
                          n_tiles=n_tiles),
        grid=(n_tiles + 1,),
        in_specs=[
            pl.BlockSpec((FFN_TM, D), lambda i: (prev(i), 0)),
            pl.BlockSpec((FFN_TM, D), lambda i: (cur(i), 0)),
            pl.BlockSpec((1, 1, N_MOD, D), lambda i: (layer, prev(i) // tiles_per_seq, 0, 0)),
            pl.BlockSpec((1, 1, N_MOD, D), lambda i: (layer, cur(i) // tiles_per_seq, 0, 0)),
            pl.BlockSpec(memory_space=pl.ANY),
            pl.BlockSpec(memory_space=pl.ANY),
            _layer_spec(ln_g, layer),
            _layer_spec(ln_b, layer),
        ],
        out_specs=pl.BlockSpec((FFN_TM, D), lambda i: (prev(i), 0)),
        out_shape=jax.ShapeDtypeStruct((M, D), F32),
        scratch_shapes=[
            pltpu.VMEM((FFN_TM, D_FF), BF16),
            pltpu.VMEM((FFN_TM, D), F32),
            pltpu.VMEM((D, 2 * D_FF), BF16),
            pltpu.VMEM((D_FF, D), BF16),
            pltpu.VMEM((2, D, FFN_TF), F32),
            pltpu.VMEM((2, D, FFN_TF), F32),
            pltpu.VMEM((2, FFN_TF, D), F32),
            pltpu.SemaphoreType.DMA((2,)),
            pltpu.SemaphoreType.DMA((2,)),
            pltpu.SemaphoreType.DMA((2,)),
        ],
        compiler_params=pltpu.CompilerParams(
            dimension_semantics=("arbitrary",),
            vmem_limit_bytes=VMEM_LIMIT),
        name="ffn",
    )(x2, x2, mod, mod, w_up, w_down, ln_g, ln_b)


def _split_bf16(t, parts):
    out = []
    r = t
    for _ in range(parts):
        p = r.astype(BF16)
        out.append(p)
        r = r - p.astype(F32)
    return out


def _mixer_tile(s, post_norm, x_ref, mod_ref, walpha_ref, balpha_ref, gng_ref, wpool_ref, pscale_ref,
                state_ref, uext_ref, qt_ref, kt_ref, kend_ref, v_ref, dect_ref, og_ref,
                rg_ref, gg_ref, gp_ref, wint_ref, wpg_ref, wpp_ref, wout_ref, y_ref):
    TS = MIX_TS
    NCH = TS // CHUNK

    @pl.when(s == 0)
    def _():
        state_ref[...] = jnp.zeros_like(state_ref)
        uext_ref[0:POOL_HALO, :] = jnp.zeros((POOL_HALO, POOL_WIDTH), F32)

    @pl.when(s > 0)
    def _():
        uext_ref[0:POOL_HALO, :] = uext_ref[TS:TS + POOL_HALO, :]

    x = x_ref[0]
    sh = mod_ref[0, 0, 3:4, :]
    sc = mod_ref[0, 0, 4:5, :]
    h = (x * (1.0 + sc) + sh).astype(BF16)

    nt = (((1,), (1,)), ((), ()))

    def proj(r0, r1):
        return lax.dot_general(h, wint_ref[r0:r1, :], nt, preferred_element_type=F32)

    a_wide = proj(IN_A, IN_A + LANES)
    v_ref[:, 0:PIECE] = proj(IN_V, IN_V + PIECE).astype(BF16)
    lane = lax.broadcasted_iota(jnp.int32, a_wide.shape, 1)
    a_lr = jnp.where(lane < GLA_RANK, a_wide, 0.0).astype(BF16)
    pre = jnp.dot(a_lr, walpha_ref[0], preferred_element_type=F32) + balpha_ref[0]
    log_a = (jnp.minimum(pre, 0.0) - jnp.log(1.0 + jnp.exp(-jnp.abs(pre)))) / GLA_TAU

    v_ref[:, PIECE:GLA_DV] = proj(IN_V + PIECE, IN_R).astype(BF16)

    ri = lax.broadcasted_iota(jnp.int32, (CUM_ROWS, CUM_ROWS), 0)
    ci = lax.broadcasted_iota(jnp.int32, (CUM_ROWS, CUM_ROWS), 1)
    tri = jnp.where((ri // CHUNK == ci // CHUNK) & (ci <= ri), 1.0, 0.0).astype(BF16)
    b_parts = []
    for g0 in range(0, TS, CUM_ROWS):
        acc = None
        for part in _split_bf16(log_a[g0:g0 + CUM_ROWS], CUM_TERMS):
            t = jnp.dot(tri, part, preferred_element_type=F32)
            acc = t if acc is None else acc + t
        b_parts.append(acc)
    b = jnp.concatenate(b_parts, axis=0)

    q = proj(IN_Q, IN_K) * (GLA_HK ** -0.5)
    k = proj(IN_K, IN_V)
    qt_ref[...] = (q * jnp.exp(b)).astype(BF16)
    kt_ref[...] = (k * jnp.exp(-b)).astype(BF16)
    dec_rows = []
    for c in range(NCH):
        r0 = c * CHUNK
        b_c = b[r0:r0 + CHUNK]
        b_last = b_c[CHUNK - 1:CHUNK, :]
        kend_ref[r0:r0 + CHUNK, :] = (k[r0:r0 + CHUNK] * jnp.exp(b_last - b_c)).astype(BF16)
        dec_rows.append(jnp.exp(b_last))
    dec_all = jnp.concatenate(
        dec_rows + [jnp.zeros((LANES - NCH, GLA_DK), F32)], axis=0)
    dect_ref[...] = dec_all.T

    def r_piece(c0):
        rg_ref[:, c0:c0 + PIECE] = _silu(proj(IN_R + c0, IN_R + c0 + PIECE))

    def u_piece(c0):
        uext_ref[POOL_HALO:POOL_HALO + TS, c0:c0 + PIECE] = proj(IN_U + c0, IN_U + c0 + PIECE)

    def gp_piece(c0):
        gp_ref[:, c0:c0 + PIECE] = jax.nn.sigmoid(proj(IN_GP + c0, IN_GP + c0 + PIECE))

    def gg_piece(c0):
        gg_ref[:, c0:c0 + PIECE] = jax.nn.sigmoid(proj(IN_GG + c0, IN_GG + c0 + PIECE))

    pieces = ([functools.partial(r_piece, c0) for c0 in range(0, GLA_DV, PIECE)]
              + [functools.partial(u_piece, c0) for c0 in range(0, POOL_WIDTH, PIECE)]
              + [functools.partial(gp_piece, c0) for c0 in range(0, D_MODEL, PIECE)])

    row = lax.broadcasted_iota(jnp.int32, (CHUNK, CHUNK), 0)
    col = lax.broadcasted_iota(jnp.int32, (CHUNK, CHUNK), 1)
    causal = col <= row
    tn = (((0,), (0,)), ((), ()))
    n_steps = NCH * GLA_HEADS
    step = 0

    def chunk_scores(c):
        out = []
        for hd in range(GLA_HEADS):
            rows, cols = slice(c * CHUNK, (c + 1) * CHUNK), slice(hd * GLA_HK, (hd + 1) * GLA_HK)
            sc = lax.dot_general(qt_ref[rows, cols], kt_ref[rows, cols], nt,
                                 preferred_element_type=F32)
            out.append(jnp.where(causal, sc, 0.0).astype(BF16))
        return out

    for _ in range(PIECES_AHEAD):
        pieces.pop(0)()
    scores_next = chunk_scores(0)
    for c in range(NCH):
        r0 = c * CHUNK
        scores_cur = scores_next
        if c + 1 < NCH:
            scores_next = chunk_scores(c + 1)
        anchor = _zero_bits_of(post_norm(c)) if c < TS // LN_ROWS else None
        for hd in range(GLA_HEADS):
            k0 = hd * GLA_HK
            v0 = hd * GLA_HV
            q_c = qt_ref[r0:r0 + CHUNK, k0:k0 + GLA_HK]
            ke_c = kend_ref[r0:r0 + CHUNK, k0:k0 + GLA_HK]
            v_c = v_ref[r0:r0 + CHUNK, v0:v0 + GLA_HV]
            st = state_ref[hd]
            scores = scores_cur[hd]
            lhs = jnp.concatenate([q_c, scores], axis=1)
            rhs = jnp.concatenate([st.astype(BF16), v_c], axis=0)
            og_ref[r0:r0 + CHUNK, v0:v0 + GLA_HV] = jnp.dot(lhs, rhs, preferred_element_type=F32)
            kv = lax.dot_general(ke_c, v_c, tn, preferred_element_type=F32)
            st_new = st * dect_ref[k0:k0 + GLA_HK, c:c + 1] + kv
            state_ref[hd] = st_new
            if anchor is not None and hd == GLA_HEADS - 1:
                state_ref[hd, 0:SUBLANES, 0:LANES] = _after(st_new[0:SUBLANES, 0:LANES], anchor)
            step += 1
            while pieces and len(pieces) * n_steps > (n_steps - step) * N_PIECES:
                pieces.pop(0)()
    while pieces:
        pieces.pop(0)()

    o_parts = []
    for hd in range(GLA_HEADS):
        v0 = hd * GLA_HV
        o_h = og_ref[:, v0:v0 + GLA_HV]
        mu = jnp.mean(o_h, axis=-1, keepdims=True)
        dlt = o_h - mu
        var = jnp.mean(dlt * dlt, axis=-1, keepdims=True)
        o_parts.append((dlt * lax.rsqrt(var + LN_EPS) * gng_ref[0, :, v0:v0 + GLA_HV]
                        * rg_ref[:, v0:v0 + GLA_HV]).astype(BF16))
        gg_piece(hd * PIECE)
    o_n = jnp.concatenate(o_parts, axis=-1)
    y_gla = jnp.dot(o_n, wpg_ref[...], preferred_element_type=F32)

    pos = s * TS + lax.broadcasted_iota(jnp.int32, (TS, POOL_GW), 0)
    p_parts = []
    for gi, w in enumerate(POOL_WINDOWS):
        c0 = gi * POOL_GW
        ext = uext_ref[:, c0:c0 + POOL_GW]
        span = 1
        while span < w:
            ext = ext + pltpu.roll(ext, span, axis=0)
            span *= 2
        wsum = ext[POOL_HALO:POOL_HALO + TS]
        u_g = uext_ref[POOL_HALO:POOL_HALO + TS, c0:c0 + POOL_GW]
        cnt = jnp.minimum(pos + 1, w).astype(F32)
        p_g = (wsum / cnt - u_g).astype(BF16)
        p_parts.append(jnp.dot(p_g, wpool_ref[0, gi], preferred_element_type=F32))
    p = (jnp.concatenate(p_parts, axis=-1) * pscale_ref[0]).astype(BF16)
    y_pool = jnp.dot(p, wpp_ref[...], preferred_element_type=F32)

    merged = gg_ref[...] * y_gla + gp_ref[...] * y_pool
    y_ref[...] = jnp.dot(merged.astype(BF16), wout_ref[...], preferred_element_type=F32)


def _mixer_kernel(xp_ref, x_ref, modp_ref, mod_ref, wint_hbm, wpg_hbm, wpp_hbm, wout_hbm,
                  walpha_ref, balpha_ref, gng_ref, wpool_ref, pscale_ref, lng_ref, lnb_ref,
                  o_ref,
                  state_ref, uext_ref, qt_ref, kt_ref, kend_ref, v_ref, dect_ref, og_ref,
                  rg_ref, gg_ref, gp_ref,
                  wint_ref, wpg_ref, wpp_ref, wout_ref, y_ref, stage_ref, sems,
                  *, layer, n_tiles, tiles_per_seq):
    i = pl.program_id(0)

    def post_norm(rb):
        rows = pl.ds(rb * LN_ROWS, LN_ROWS)
        gt = modp_ref[0, 0, 5:6, :]
        t = ALPHA * xp_ref[0, rows, :] + (1.0 + gt) * y_ref[rows, :]
        out = _layer_norm(t, lng_ref[0, 1:2, :], lnb_ref[0, 1:2, :])
        o_ref[0, rows, :] = out
        return out

    @pl.when(i == 0)
    def _():
        y_ref[...] = jnp.zeros_like(y_ref)
        _stage_bf16(
            _row_windows(wint_hbm.at[layer], wint_ref, IN_WIDTH, RING_ROWS)
            + _row_windows(wpg_hbm.at[layer], wpg_ref, GLA_DV, RING_ROWS)
            + _row_windows(wpp_hbm.at[layer], wpp_ref, POOL_WIDTH, RING_ROWS)
            + _row_windows(wout_hbm.at[layer], wout_ref, D_MODEL, RING_ROWS),
            stage_ref, sems)

    @pl.when(i < n_tiles)
    def _():
        _mixer_tile(i % tiles_per_seq, post_norm, x_ref, mod_ref, walpha_ref, balpha_ref, gng_ref,
                    wpool_ref, pscale_ref, state_ref, uext_ref, qt_ref, kt_ref, kend_ref, v_ref,
                    dect_ref, og_ref, rg_ref, gg_ref, gp_ref, wint_ref, wpg_ref, wpp_ref, wout_ref,
                    y_ref)

    @pl.when(i == n_tiles)
    def _():
        for rb in range(MIX_TS // LN_ROWS):
            post_norm(rb)


def _mixer_call(x, mod, big_weights, small_operands, *, layer):
    B, S, D = x.shape
    TS = MIX_TS
    tiles_per_seq = S // TS
    n_tiles = B * tiles_per_seq

    def cur(i):
        return jnp.minimum(i, n_tiles - 1)

    def prev(i):
        return jnp.maximum(i - 1, 0)

    def x_map(tile):
        return lambda i: (tile(i) // tiles_per_seq, tile(i) % tiles_per_seq, 0)

    def mod_map(tile):
        return lambda i: (layer, tile(i) // tiles_per_seq, 0, 0)

    return pl.pallas_call(
        functools.partial(_mixer_kernel, layer=layer, n_tiles=n_tiles, tiles_per_seq=tiles_per_seq),
        grid=(n_tiles + 1,),
        in_specs=[
            pl.BlockSpec((1, TS, D), x_map(prev)),
            pl.BlockSpec((1, TS, D), x_map(cur)),
            pl.BlockSpec((1, 1, N_MOD, D), mod_map(prev)),
            pl.BlockSpec((1, 1, N_MOD, D), mod_map(cur)),
        ] + [pl.BlockSpec(memory_space=pl.ANY) for _ in big_weights]
          + [_layer_spec(a, layer) for a in small_operands],
        out_specs=pl.BlockSpec((1, TS, D), x_map(prev)),
        out_shape=jax.ShapeDtypeStruct((B, S, D), F32),
        scratch_shapes=[
            pltpu.VMEM((GLA_HEADS, GLA_HK, GLA_HV), F32),
            pltpu.VMEM((POOL_HALO + TS, POOL_WIDTH), F32),
            pltpu.VMEM((TS, GLA_DK), BF16),
            pltpu.VMEM((TS, GLA_DK), BF16),
            pltpu.VMEM((TS, GLA_DK), BF16),
            pltpu.VMEM((TS, GLA_DV), BF16),
            pltpu.VMEM((GLA_DK, LANES), F32),
            pltpu.VMEM((TS, GLA_DV), F32),
            pltpu.VMEM((TS, GLA_DV), F32),
            pltpu.VMEM((TS, D_MODEL), F32),
            pltpu.VMEM((TS, D_MODEL), F32),
            pltpu.VMEM((IN_WIDTH, D_MODEL), BF16),
            pltpu.VMEM((GLA_DV, D_MODEL), BF16),
            pltpu.VMEM((POOL_WIDTH, D_MODEL), BF16),
            pltpu.VMEM((D_MODEL, D_MODEL), BF16),
            pltpu.VMEM((TS, D_MODEL), F32),
            pltpu.VMEM((RING_SLOTS, RING_ROWS, D_MODEL), F32),
            pltpu.SemaphoreType.DMA((RING_SLOTS,)),
        ],
        compiler_params=pltpu.CompilerParams(
            dimension_semantics=("arbitrary",),
            vmem_limit_bytes=VMEM_LIMIT),
        name="mixer",
    )(x, x, mod, mod, *big_weights, *small_operands)


def _mixer_slab_operands(w_alpha, b_alpha, gla_norm_g, w_pool, pool_scale, ln_g, ln_b):
    L = w_alpha.shape[0]
    w_al = jnp.pad(w_alpha, ((0, 0), (0, LANES - GLA_RANK), (0, 0))).astype(BF16)
    return (w_al, b_alpha.reshape(L, 1, GLA_DK), gla_norm_g.reshape(L, 1, GLA_DV),
            w_pool.astype(BF16), pool_scale.reshape(L, 1, POOL_WIDTH), ln_g, ln_b)


def kernel(x, c, w_ada, b_ada, ffn1_up, ffn1_down, w_in, w_alpha, b_alpha, gla_norm_g,
           w_proj_gla, w_pool, pool_scale, w_proj_pool, w_out, ffn2_up, ffn2_down, ln_g, ln_b):
    B, S, D = x.shape
    L = w_ada.shape[0]
    c_pad = jnp.pad(c, ((0, SUBLANES - B), (0, 0)))
    mod = _ada_call(c_pad, w_ada, b_ada)[:, :B].reshape(L, B, N_MOD, D)
    mix_big = (jnp.swapaxes(w_in, 1, 2), w_proj_gla, w_proj_pool, w_out)
    mix_small = _mixer_slab_operands(w_alpha, b_alpha, gla_norm_g, w_pool, pool_scale, ln_g, ln_b)
    x2 = x.reshape(B * S, D)
    for l in range(L):
        x2 = _ffn_call(x2, mod, ffn1_up, ffn1_down, ln_g, ln_b, layer=l, mod_base=0, ln_row=0,
                       seq=S)
        x3 = _mixer_call(x2.reshape(B, S, D), mod, mix_big, mix_small, layer=l)
        x2 = _ffn_call(x3.reshape(B * S, D), mod, ffn2_up, ffn2_down, ln_g, ln_b, layer=l,
                       mod_base=6, ln_row=2, seq=S)
    return x2.reshape(B, S, D)
```

```python
import functools

import jax
import jax.numpy as jnp
from jax import lax
from jax.experimental import pallas as pl
from jax.experimental.pallas import tpu as pltpu

F32 = jnp.float32
BF16 = jnp.bfloat16

D_MODEL = 1024
DEPTH = 4
GLA_HEADS = 4
GLA_DK = 512
GLA_DV = 1024
GLA_HK = 128
GLA_HV = 256
GLA_RANK = 16
GLA_TAU = 16.0
CHUNK = 64
POOL_WIDTH = 512
POOL_WINDOWS = (2, 4, 8, 16)
POOL_GW = 128
POOL_HALO = 16
D_FF = 2816
N_MOD = 9
ALPHA = (2 * DEPTH) ** 0.25
LN_EPS = 1e-5
LANES = 128
SUBLANES = 8
BF16_ROWS = 16
V7X_VMEM_BYTES = 64 * 1024 * 1024

FFN_TM = 512
FFN_TF = 256
MIX_TS = 512
LN_ROWS = 64
RING_ROWS = 512
RING_SLOTS = 4
CUM_ROWS = 256
ADA_TN = 2304
VMEM_LIMIT = V7X_VMEM_BYTES * 7 // 8

IN_Q = 0
IN_K = IN_Q + GLA_DK
IN_V = IN_K + GLA_DK
IN_R = IN_V + GLA_DV
IN_A = IN_R + GLA_DV
IN_U = IN_A + GLA_RANK
IN_GG = IN_U + POOL_WIDTH
IN_GP = IN_GG + D_MODEL
IN_WIDTH = IN_GP + D_MODEL
PIECE = 256
PIECES_AHEAD = 2
N_PIECES = (GLA_DV + POOL_WIDTH + D_MODEL) // PIECE
CUM_TERMS = 2


def _layer_norm(t, g, b):
    mu = jnp.mean(t, axis=-1, keepdims=True)
    d = t - mu
    var = jnp.mean(d * d, axis=-1, keepdims=True)
    return d * lax.rsqrt(var + LN_EPS) * g + b


def _silu(t):
    return t * jax.nn.sigmoid(t)


def _zero_bits_of(v):
    bits = lax.bitcast_convert_type(v, jnp.uint32)
    acc = None
    for r0 in range(0, v.shape[0], SUBLANES):
        for c0 in range(0, v.shape[1], LANES):
            tile = bits[r0:r0 + SUBLANES, c0:c0 + LANES]
            acc = tile if acc is None else acc | tile
    return lax.shift_right_logical(lax.shift_right_logical(acc, jnp.uint32(16)), jnp.uint32(16))


def _after(v, zero_bits):
    reps = v.shape[0] // zero_bits.shape[0]
    z = jnp.concatenate([zero_bits] * reps, axis=0) if reps > 1 else zero_bits
    return jnp.where(z == 0, v, 0.0)


def _window_stream(windows, stage_ref, sems):
    n_slots = stage_ref.shape[0]

    def copy(j):
        src, _ = windows[j]
        slot = stage_ref.at[j % n_slots, 0:src.shape[0], 0:src.shape[1]]
        return pltpu.make_async_copy(src, slot, sems.at[j % n_slots])

    def start(j):
        if j < len(windows):
            copy(j).start()

    def wait(j):
        copy(j).wait()

    def convert(j):
        src, dst = windows[j]
        dst[...] = stage_ref[j % n_slots, 0:src.shape[0], 0:src.shape[1]].astype(BF16)

    return start, wait, convert


def _row_windows(src, dst, n_rows, step):
    return [(src.at[pl.ds(r0, min(step, n_rows - r0)), :], dst.at[pl.ds(r0, min(step, n_rows - r0)), :])
            for r0 in range(0, n_rows, step)]


def _layer_spec(a, layer):
    nd = a.ndim
    return pl.BlockSpec((1,) + a.shape[1:], lambda *_: (layer,) + (0,) * (nd - 1),
                        pipeline_mode=pl.Buffered(1))


def _ada_kernel(c_ref, w_ref, b_ref, o_ref):
    c_act = _silu(c_ref[...]).astype(BF16)
    w = w_ref[0].astype(BF16)
    o_ref[0] = jnp.dot(c_act, w, preferred_element_type=F32) + b_ref[0]


def _ada_call(c_pad, w_ada, b_ada):
    L, D, N = w_ada.shape
    rows = c_pad.shape[0]
    return pl.pallas_call(
        _ada_kernel,
        grid=(L, N // ADA_TN),
        in_specs=[
            pl.BlockSpec((rows, D), lambda l, n: (0, 0)),
            pl.BlockSpec((1, D, ADA_TN), lambda l, n: (l, 0, n)),
            pl.BlockSpec((1, 1, ADA_TN), lambda l, n: (l, 0, n)),
        ],
        out_specs=pl.BlockSpec((1, rows, ADA_TN), lambda l, n: (l, 0, n)),
        out_shape=jax.ShapeDtypeStruct((L, rows, N), F32),
        compiler_params=pltpu.CompilerParams(
            dimension_semantics=("arbitrary", "arbitrary"),
            vmem_limit_bytes=VMEM_LIMIT),
        name="ada_mod",
    )(c_pad, w_ada, b_ada.reshape(L, 1, N))


def _ffn_kernel(xp_ref, x_ref, modp_ref, mod_ref, wup_hbm, wdn_hbm, lng_ref, lnb_ref,
                o_ref, a_ref, y_ref, wup_ref, wdn_ref, gstage_ref, ustage_ref, dstage_ref,
                g_sems, u_sems, d_sems, *, layer, mod_base, ln_row, n_tiles):
    i = pl.program_id(0)
    chunks = range(0, D_FF, FFN_TF)
    streams = [
        _window_stream([(wup_hbm.at[layer, :, pl.ds(c0, FFN_TF)], wup_ref.at[:, pl.ds(c0, FFN_TF)])
                        for c0 in chunks], gstage_ref, g_sems),
        _window_stream([(wup_hbm.at[layer, :, pl.ds(D_FF + c0, FFN_TF)],
                         wup_ref.at[:, pl.ds(D_FF + c0, FFN_TF)]) for c0 in chunks],
                       ustage_ref, u_sems),
        _window_stream(_row_windows(wdn_hbm.at[layer], wdn_ref, D_FF, FFN_TF), dstage_ref, d_sems),
    ]

    def post_norm(rb):
        rows = pl.ds(rb * LN_ROWS, LN_ROWS)
        gt = modp_ref[0, 0, mod_base + 2:mod_base + 3, :]
        t = ALPHA * xp_ref[rows, :] + (0.5 * (1.0 + gt)) * y_ref[rows, :]
        out = _layer_norm(t, lng_ref[0, ln_row:ln_row + 1, :], lnb_ref[0, ln_row:ln_row + 1, :])
        o_ref[rows, :] = out
        return out

    def tile(stage_weights):
        if stage_weights:
            for start, _, _ in streams:
                start(0)
                start(1)
        sh = mod_ref[0, 0, mod_base:mod_base + 1, :]
        sc = mod_ref[0, 0, mod_base + 1:mod_base + 2, :]
        h = (x_ref[...] * (1.0 + sc) + sh).astype(BF16)
        anchor = None
        for ci, c0 in enumerate(chunks):
            if stage_weights:
                for _, wait, _ in streams:
                    wait(ci)
                for start, _, convert in streams:
                    convert(ci)
                    start(ci + 2)
            g = jnp.dot(h, wup_ref[:, c0:c0 + FFN_TF], preferred_element_type=F32)
            u = jnp.dot(h, wup_ref[:, D_FF + c0:D_FF + c0 + FFN_TF], preferred_element_type=F32)
            act = _silu(g) * u
            a_ref[:, c0:c0 + FFN_TF] = act.astype(BF16)
            if anchor is not None:
                a_ref[0:BF16_ROWS, c0:c0 + LANES] = _after(act[0:BF16_ROWS, 0:LANES],
                                                           anchor).astype(BF16)
            anchor = _zero_bits_of(post_norm(ci)) if ci < FFN_TM // LN_ROWS else None
        y_ref[...] = jnp.dot(a_ref[...], wdn_ref[...], preferred_element_type=F32)

    @pl.when(i == 0)
    def _():
        y_ref[...] = jnp.zeros_like(y_ref)
        tile(stage_weights=True)

    @pl.when((i > 0) & (i < n_tiles))
    def _():
        tile(stage_weights=False)

    @pl.when(i == n_tiles)
    def _():
        for rb in range(FFN_TM // LN_ROWS):
            post_norm(rb)


def _ffn_call(x2, mod, w_up, w_down, ln_g, ln_b, *, layer, mod_base, ln_row, seq):
    M, D = x2.shape
    n_tiles = M // FFN_TM
    tiles_per_seq = seq // FFN_TM
    assert D_FF // FFN_TF >= FFN_TM // LN_ROWS

    def cur(i):
        return jnp.minimum(i, n_tiles - 1)

    def prev(i):
        return jnp.maximum(i - 1, 0)

    return pl.pallas_call(
        functools.partial(_ffn_kernel, layer=layer, mod_base=mod_base, ln_row=ln_row,
                          n_tiles=n_tiles),
        grid=(n_tiles + 1,),
        in_specs=[
            pl.BlockSpec((FFN_TM, D), lambda i: (prev(i), 0)),
            pl.BlockSpec((FFN_TM, D), lambda i: (cur(i), 0)),
            pl.BlockSpec((1, 1, N_MOD, D), lambda i: (layer, prev(i) // tiles_per_seq, 0, 0)),
            pl.BlockSpec((1, 1, N_MOD, D), lambda i: (layer, cur(i) // tiles_per_seq, 0, 0)),
            pl.BlockSpec(memory_space=pl.ANY),
            pl.BlockSpec(memory_space=pl.ANY),
            _layer_spec(ln_g, layer),
            _layer_spec(ln_b, layer),
        ],
        out_specs=pl.BlockSpec((FFN_TM, D), lambda i: (prev(i), 0)),
        out_shape=jax.ShapeDtypeStruct((M, D), F32),
        scratch_shapes=[
            pltpu.VMEM((FFN_TM, D_FF), BF16),
            pltpu.VMEM((FFN_TM, D), F32),
            pltpu.VMEM((D, 2 * D_FF), BF16),
            pltpu.VMEM((D_FF, D), BF16),
            pltpu.VMEM((2, D, FFN_TF), F32),
            pltpu.VMEM((2, D, FFN_TF), F32),
            pltpu.VMEM((2, FFN_TF, D), F32),
            pltpu.SemaphoreType.DMA((2,)),
            pltpu.SemaphoreType.DMA((2,)),
            pltpu.SemaphoreType.DMA((2,)),
        ],
        compiler_params=pltpu.CompilerParams(
            dimension_semantics=("arbitrary",),
            vmem_limit_bytes=VMEM_LIMIT),
        name="ffn",
    )(x2, x2, mod, mod, w_up, w_down, ln_g, ln_b)


def _split_bf16(t, parts):
    out = []
    r = t
    for _ in range(parts):
        p = r.astype(BF16)
        out.append(p)
        r = r - p.astype(F32)
    return out


def _mixer_stager(layer, wint_hbm, wpg_hbm, wpp_hbm, wout_hbm, wint_ref, wpg_ref, wpp_ref, wout_ref,
                  stage_ref, sems):
    groups = (
        ("first", wint_hbm, wint_ref, IN_A, IN_GG - IN_A), ("first", wint_hbm, wint_ref, IN_V, GLA_DV),
        ("first", wint_hbm, wint_ref, IN_Q, 2 * GLA_DK),
        ("pieces", wint_hbm, wint_ref, IN_R, GLA_DV), ("pieces", wint_hbm, wint_ref, IN_GP, D_MODEL),
        ("gla_out", wint_hbm, wint_ref, IN_GG, D_MODEL), ("gla_out", wpg_hbm, wpg_ref, 0, GLA_DV),
        ("merge", wpp_hbm, wpp_ref, 0, POOL_WIDTH), ("merge", wout_hbm, wout_ref, 0, D_MODEL))
    windows, end_of = [], {}
    for name, src, dst, row0, rows in groups:
        for r in range(row0, row0 + rows, RING_ROWS):
            n = min(RING_ROWS, row0 + rows - r)
            windows.append((src.at[layer, pl.ds(r, n), :], dst.at[pl.ds(r, n), :]))
        end_of[name] = len(windows)
    start, wait, convert = _window_stream(windows, stage_ref, sems)
    n_slots = stage_ref.shape[0]
    done = [0]

    def begin():
        for j in range(n_slots):
            start(j)

    def need(name):
        for j in range(done[0], end_of[name]):
            wait(j)
            convert(j)
            start(j + n_slots)
        done[0] = max(done[0], end_of[name])

    return begin, need


def _mixer_tile(s, post_norm, need, x_ref, mod_ref, walpha_ref, balpha_ref, gng_ref, wpool_ref,
                pscale_ref, state_ref, uext_ref, qt_ref, kt_ref, kend_ref, v_ref, dect_ref, og_ref,
                rg_ref, gg_ref, gp_ref, wint_ref, wpg_ref, wpp_ref, wout_ref, y_ref):
    TS = MIX_TS
    NCH = TS // CHUNK

    @pl.when(s == 0)
    def _():
        state_ref[...] = jnp.zeros_like(state_ref)
        uext_ref[0:POOL_HALO, :] = jnp.zeros((POOL_HALO, POOL_WIDTH), F32)

    @pl.when(s > 0)
    def _():
        uext_ref[0:POOL_HALO, :] = uext_ref[TS:TS + POOL_HALO, :]

    need("first")
    x = x_ref[0]
    sh = mod_ref[0, 0, 3:4, :]
    sc = mod_ref[0, 0, 4:5, :]
    h = (x * (1.0 + sc) + sh).astype(BF16)

    nt =(((1,), (1,)), ((), ()))

    def proj(r0, r1):
        return lax.dot_general(h, wint_ref[r0:r1, :], nt, preferred_element_type=F32)

    a_wide = proj(IN_A, IN_A + LANES)
    v_ref[:, 0:PIECE] = proj(IN_V, IN_V + PIECE).astype(BF16)
    lane = lax.broadcasted_iota(jnp.int32, a_wide.shape, 1)
    a_lr = jnp.where(lane < GLA_RANK, a_wide, 0.0).astype(BF16)
    pre = jnp.dot(a_lr, walpha_ref[0], preferred_element_type=F32) + balpha_ref[0]
    log_a = (jnp.minimum(pre, 0.0) - jnp.log(1.0 + jnp.exp(-jnp.abs(pre)))) / GLA_TAU

    v_ref[:, PIECE:GLA_DV] = proj(IN_V + PIECE, IN_R).astype(BF16)

    ri = lax.broadcasted_iota(jnp.int32, (CUM_ROWS, CUM_ROWS), 0)
    ci = lax.broadcasted_iota(jnp.int32, (CUM_ROWS, CUM_ROWS), 1)
    tri = jnp.where((ri // CHUNK == ci // CHUNK) & (ci <= ri), 1.0, 0.0).astype(BF16)
    b_parts = []
    for g0 in range(0, TS, CUM_ROWS):
        acc = None
        for part in _split_bf16(log_a[g0:g0 + CUM_ROWS], CUM_TERMS):
            t = jnp.dot(tri, part, preferred_element_type=F32)
            acc = t if acc is None else acc + t
        b_parts.append(acc)
    b = jnp.concatenate(b_parts, axis=0)

    q = proj(IN_Q, IN_K) * (GLA_HK ** -0.5)
    k = proj(IN_K, IN_V)
    qt_ref[...] = (q * jnp.exp(b)).astype(BF16)
    kt_ref[...] = (k * jnp.exp(-b)).astype(BF16)
    dec_rows = []
    for c in range(NCH):
        r0 = c * CHUNK
        b_c = b[r0:r0 + CHUNK]
        b_last = b_c[CHUNK - 1:CHUNK, :]
        kend_ref[r0:r0 + CHUNK, :] = (k[r0:r0 + CHUNK] * jnp.exp(b_last - b_c)).astype(BF16)
        dec_rows.append(jnp.exp(b_last))
    dec_all = jnp.concatenate(
        dec_rows + [jnp.zeros((LANES - NCH, GLA_DK), F32)], axis=0)
    dect_ref[...] = dec_all.T

    def r_piece(c0):
        rg_ref[:, c0:c0 + PIECE] = _silu(proj(IN_R + c0, IN_R + c0 + PIECE))

    def u_piece(c0):
        uext_ref[POOL_HALO:POOL_HALO + TS, c0:c0 + PIECE] = proj(IN_U + c0, IN_U + c0 + PIECE)

    def gp_piece(c0):
        gp_ref[:, c0:c0 + PIECE] = jax.nn.sigmoid(proj(IN_GP + c0, IN_GP + c0 + PIECE))

    def gg_piece(c0):
        gg_ref[:, c0:c0 + PIECE] = jax.nn.sigmoid(proj(IN_GG + c0, IN_GG + c0 + PIECE))

    pieces = ([functools.partial(r_piece, c0) for c0 in range(0, GLA_DV, PIECE)]
              + [functools.partial(u_piece, c0) for c0 in range(0, POOL_WIDTH, PIECE)]
              + [functools.partial(gp_piece, c0) for c0 in range(0, D_MODEL, PIECE)])

    row = lax.broadcasted_iota(jnp.int32, (CHUNK, CHUNK), 0)
    col = lax.broadcasted_iota(jnp.int32, (CHUNK, CHUNK), 1)
    causal = col <= row
    tn = (((0,), (0,)), ((), ()))
    n_steps = NCH * GLA_HEADS
    step = 0

    def chunk_scores(c):
        out = []
        for hd in range(GLA_HEADS):
            rows, cols = slice(c * CHUNK, (c + 1) * CHUNK), slice(hd * GLA_HK, (hd + 1) * GLA_HK)
            sc = lax.dot_general(qt_ref[rows, cols], kt_ref[rows, cols], nt,
                                 preferred_element_type=F32)
            out.append(jnp.where(causal, sc, 0.0).astype(BF16))
        return out

    need("pieces")
    for _ in range(PIECES_AHEAD):
        pieces.pop(0)()
    scores_next = chunk_scores(0)
    for c in range(NCH):
        r0 = c * CHUNK
        scores_cur = scores_next
        if c + 1 < NCH:
            scores_next = chunk_scores(c + 1)
        anchor = _zero_bits_of(post_norm(c)) if c < TS // LN_ROWS else None
        for hd in range(GLA_HEADS):
            k0 = hd * GLA_HK
            v0 = hd * GLA_HV
            q_c = qt_ref[r0:r0 + CHUNK, k0:k0 + GLA_HK]
            ke_c = kend_ref[r0:r0 + CHUNK, k0:k0 + GLA_HK]
            v_c = v_ref[r0:r0 + CHUNK, v0:v0 + GLA_HV]
            st = state_ref[hd]
            scores = scores_cur[hd]
            lhs = jnp.concatenate([q_c, scores], axis=1)
            rhs = jnp.concatenate([st.astype(BF16), v_c], axis=0)
            og_ref[r0:r0 + CHUNK, v0:v0 + GLA_HV] = jnp.dot(lhs, rhs, preferred_element_type=F32)
            kv = lax.dot_general(ke_c, v_c, tn, preferred_element_type=F32)
            st_new = st * dect_ref[k0:k0 + GLA_HK, c:c + 1] + kv
            state_ref[hd] = st_new
            if anchor is not None and hd == GLA_HEADS - 1:
                state_ref[hd, 0:SUBLANES, 0:LANES] = _after(st_new[0:SUBLANES, 0:LANES], anchor)
            step += 1
            while pieces and len(pieces) * n_steps > (n_steps - step) * N_PIECES:
                pieces.pop(0)()
    while pieces:
        pieces.pop(0)()

    need("gla_out")
    o_parts = []
    for hd in range(GLA_HEADS):
        v0 = hd * GLA_HV
        o_h = og_ref[:, v0:v0 + GLA_HV]
        mu = jnp.mean(o_h, axis=-1, keepdims=True)
        dlt = o_h - mu
        var = jnp.mean(dlt * dlt, axis=-1, keepdims=True)
        o_parts.append((dlt * lax.rsqrt(var + LN_EPS) * gng_ref[0, :, v0:v0 + GLA_HV]
                        * rg_ref[:, v0:v0 + GLA_HV]).astype(BF16))
        gg_piece(hd * PIECE)
    o_n = jnp.concatenate(o_parts, axis=-1)
    y_gla = jnp.dot(o_n, wpg_ref[...], preferred_element_type=F32)

    pos = s * TS + lax.broadcasted_iota(jnp.int32, (TS, POOL_GW), 0)
    p_parts = []
    for gi, w in enumerate(POOL_WINDOWS):
        c0 = gi * POOL_GW
        ext = uext_ref[:, c0:c0 + POOL_GW]
        span = 1
        while span < w:
            ext = ext + pltpu.roll(ext, span, axis=0)
            span *= 2
        wsum = ext[POOL_HALO:POOL_HALO + TS]
        u_g = uext_ref[POOL_HALO:POOL_HALO + TS, c0:c0 + POOL_GW]
        cnt = jnp.minimum(pos + 1, w).astype(F32)
        p_g = (wsum / cnt - u_g).astype(BF16)
        p_parts.append(jnp.dot(p_g, wpool_ref[0, gi], preferred_element_type=F32))
    p = (jnp.concatenate(p_parts, axis=-1) * pscale_ref[0]).astype(BF16)
    need("merge")
    y_pool = jnp.dot(p, wpp_ref[...], preferred_element_type=F32)

    merged = gg_ref[...] * y_gla + gp_ref[...] * y_pool
    y_ref[...] = jnp.dot(merged.astype(BF16), wout_ref[...], preferred_element_type=F32)


def _mixer_kernel(xp_ref, x_ref, modp_ref, mod_ref, wint_hbm, wpg_hbm, wpp_hbm, wout_hbm,
                  walpha_ref, balpha_ref, gng_ref, wpool_ref, pscale_ref, lng_ref, lnb_ref,
                  o_ref,
                  state_ref, uext_ref, qt_ref, kt_ref, kend_ref, v_ref, dect_ref, og_ref,
                  rg_ref, gg_ref, gp_ref,
                  wint_ref, wpg_ref, wpp_ref, wout_ref, y_ref, stage_ref, sems,
                  *, layer, n_tiles, tiles_per_seq):
    i = pl.program_id(0)

    def post_norm(rb):
        rows = pl.ds(rb * LN_ROWS, LN_ROWS)
        gt = modp_ref[0, 0, 5:6, :]
        t = ALPHA * xp_ref[0, rows, :] + (1.0 + gt) * y_ref[rows, :]
        out = _layer_norm(t, lng_ref[0, 1:2, :], lnb_ref[0, 1:2, :])
        o_ref[0, rows, :] = out
        return out

    def tile(need):
        _mixer_tile(i % tiles_per_seq, post_norm, need, x_ref, mod_ref, walpha_ref, balpha_ref,
                    gng_ref, wpool_ref, pscale_ref, state_ref, uext_ref, qt_ref, kt_ref, kend_ref,
                    v_ref, dect_ref, og_ref, rg_ref, gg_ref, gp_ref, wint_ref, wpg_ref, wpp_ref,
                    wout_ref, y_ref)

    @pl.when(i == 0)
    def _():
        y_ref[...] = jnp.zeros_like(y_ref)
        begin, need = _mixer_stager(layer, wint_hbm, wpg_hbm, wpp_hbm, wout_hbm, wint_ref, wpg_ref,
                                    wpp_ref, wout_ref, stage_ref, sems)
        begin()
        tile(need)

    @pl.when((i > 0) & (i < n_tiles))
    def _():
        tile(lambda group: None)

    @pl.when(i == n_tiles)
    def _():
        for rb in range(MIX_TS // LN_ROWS):
            post_norm(rb)


def _mixer_call(x, mod, big_weights, small_operands, *, layer):
    B, S, D = x.shape
    TS = MIX_TS
    tiles_per_seq = S // TS
    n_tiles = B * tiles_per_seq

    def cur(i):
        return jnp.minimum(i, n_tiles - 1)

    def prev(i):
        return jnp.maximum(i - 1, 0)

    def x_map(tile):
        return lambda i: (tile(i) // tiles_per_seq, tile(i) % tiles_per_seq, 0)

    def mod_map(tile):
        return lambda i: (layer, tile(i) // tiles_per_seq, 0, 0)

    return pl.pallas_call(
        functools.partial(_mixer_kernel, layer=layer, n_tiles=n_tiles, tiles_per_seq=tiles_per_seq),
        grid=(n_tiles + 1,),
        in_specs=[
            pl.BlockSpec((1, TS, D), x_map(prev)),
            pl.BlockSpec((1, TS, D), x_map(cur)),
            pl.BlockSpec((1, 1, N_MOD, D), mod_map(prev)),
            pl.BlockSpec((1, 1, N_MOD, D), mod_map(cur)),
        ] + [pl.BlockSpec(memory_space=pl.ANY) for _ in big_weights]
          + [_layer_spec(a, layer) for a in small_operands],
        out_specs=pl.BlockSpec((1, TS, D), x_map(prev)),
        out_shape=jax.ShapeDtypeStruct((B, S, D), F32),
        scratch_shapes=[
            pltpu.VMEM((GLA_HEADS, GLA_HK, GLA_HV), F32),
            pltpu.VMEM((POOL_HALO + TS, POOL_WIDTH), F32),
            pltpu.VMEM((TS, GLA_DK), BF16),
            pltpu.VMEM((TS, GLA_DK), BF16),
            pltpu.VMEM((TS, GLA_DK), BF16),
            pltpu.VMEM((TS, GLA_DV), BF16),
            pltpu.VMEM((GLA_DK, LANES), F32),
            pltpu.VMEM((TS, GLA_DV), F32),
            pltpu.VMEM((TS, GLA_DV), F32),
            pltpu.VMEM((TS, D_MODEL), F32),
            pltpu.VMEM((TS, D_MODEL), F32),
            pltpu.VMEM((IN_WIDTH, D_MODEL), BF16),
            pltpu.VMEM((GLA_DV, D_MODEL), BF16),
            pltpu.VMEM((POOL_WIDTH, D_MODEL), BF16),
            pltpu.VMEM((D_MODEL, D_MODEL), BF16),
            pltpu.VMEM((TS, D_MODEL), F32),
            pltpu.VMEM((RING_SLOTS, RING_ROWS, D_MODEL), F32),
            pltpu.SemaphoreType.DMA((RING_SLOTS,)),
        ],
        compiler_params=pltpu.CompilerParams(
            dimension_semantics=("arbitrary",),
            vmem_limit_bytes=VMEM_LIMIT),
        name="mixer",
    )(x, x, mod, mod, *big_weights, *small_operands)


def _mixer_slab_operands(w_alpha, b_alpha, gla_norm_g, w_pool, pool_scale, ln_g, ln_b):
    L = w_alpha.shape[0]
    w_al = jnp.pad(w_alpha, ((0, 0), (0, LANES - GLA_RANK), (0, 0))).astype(BF16)
    return (w_al, b_alpha.reshape(L, 1, GLA_DK), gla_norm_g.reshape(L, 1, GLA_DV),
            w_pool.astype(BF16), pool_scale.reshape(L, 1, POOL_WIDTH), ln_g, ln_b)


def kernel(x, c, w_ada, b_ada, ffn1_up, ffn1_down, w_in, w_alpha, b_alpha, gla_norm_g,
           w_proj_gla, w_pool, pool_scale, w_proj_pool, w_out, ffn2_up, ffn2_down, ln_g, ln_b):
    B, S, D = x.shape
    L = w_ada.shape[0]
    c_pad = jnp.pad(c, ((0, SUBLANES - B), (0, 0)))
    mod = _ada_call(c_pad, w_ada, b_ada)[:, :B].reshape(L, B, N_MOD, D)
    mix_big = (jnp.swapaxes(w_in, 1, 2), w_proj_gla, w_proj_pool, w_out)
    mix_small = _mixer_slab_operands(w_alpha, b_alpha, gla_norm_g, w_pool, pool_scale, ln_g, ln_b)
    x2 = x.reshape(B * S, D)
    for l in range(L):
        x2 = _ffn_call(x2, mod, ffn1_up, ffn1_down, ln_g, ln_b, layer=l, mod_base=0, ln_row=0,
                       seq=S)
        x3 = _mixer_call(x2.reshape(B, S, D), mod, mix_big, mix_small, layer=l)
        x2 = _ffn_call(x3.reshape(B * S, D), mod, ffn2_up, ffn2_down, ln_g, ln_b, layer=l,
                       mod_base=6, ln_row=2, seq=S)
    return x2.reshape(B, S, D)
```

```python
import functools

import jax
import jax.numpy as jnp
from jax import lax
from jax.experimental import pallas as pl
from jax.experimental.pallas import tpu as pltpu

F32 = jnp.float32
BF16 = jnp.bfloat16

D_MODEL = 1024
DEPTH = 4
GLA_HEADS = 4
GLA_DK = 512
GLA_DV = 1024
GLA_HK = 128
GLA_HV = 256
GLA_RANK = 16
GLA_TAU = 16.0
CHUNK = 64
POOL_WIDTH = 512
POOL_WINDOWS = (2, 4, 8, 16)
POOL_GW = 128
POOL_HALO = 16
D_FF = 2816
N_MOD = 9
ALPHA = (2 * DEPTH) ** 0.25
LN_EPS = 1e-5
LANES = 128
SUBLANES = 8
BF16_ROWS = 16
V7X_VMEM_BYTES = 64 * 1024 * 1024

FFN_TM = 512
FFN_TF = 256
MIX_TS = 512
LN_ROWS = 64
RING_ROWS = 512
RING_SLOTS = 3
CUM_ROWS = 256
ADA_TN = 2304
VMEM_LIMIT = V7X_VMEM_BYTES * 7 // 8

IN_Q = 0
IN_K = IN_Q + GLA_DK
IN_V = IN_K + GLA_DK
IN_R = IN_V + GLA_DV
IN_A = IN_R + GLA_DV
IN_U = IN_A + GLA_RANK
IN_GG = IN_U + POOL_WIDTH
IN_GP = IN_GG + D_MODEL
IN_WIDTH = IN_GP + D_MODEL
PIECE = 256
PIECES_AHEAD = 2
N_PIECES = (GLA_DV + POOL_WIDTH + D_MODEL) // PIECE
CUM_TERMS = 2


def _layer_norm(t, g, b):
    mu = jnp.mean(t, axis=-1, keepdims=True)
    d = t - mu
    var = jnp.mean(d * d, axis=-1, keepdims=True)
    return d * lax.rsqrt(var + LN_EPS) * g + b


def _silu(t):
    return t * jax.nn.sigmoid(t)


def _zero_bits_of(v):
    bits = lax.bitcast_convert_type(v, jnp.uint32)
    acc = None
    for r0 in range(0, v.shape[0], SUBLANES):
        for c0 in range(0, v.shape[1], LANES):
            tile = bits[r0:r0 + SUBLANES, c0:c0 + LANES]
            acc = tile if acc is None else acc | tile
    return lax.shift_right_logical(lax.shift_right_logical(acc, jnp.uint32(16)), jnp.uint32(16))


def _after(v, zero_bits):
    reps = v.shape[0] // zero_bits.shape[0]
    z = jnp.concatenate([zero_bits] * reps, axis=0) if reps > 1 else zero_bits
    return jnp.where(z == 0, v, 0.0)


def _window_stream(windows, stage_ref, sems):
    n_slots = stage_ref.shape[0]

    def copy(j):
        src, _ = windows[j]
        slot = stage_ref.at[j % n_slots, 0:src.shape[0], 0:src.shape[1]]
        return pltpu.make_async_copy(src, slot, sems.at[j % n_slots])

    def start(j):
        if j < len(windows):
            copy(j).start()

    def wait(j):
        copy(j).wait()

    def convert(j):
        src, dst = windows[j]
        dst[...] = stage_ref[j % n_slots, 0:src.shape[0], 0:src.shape[1]].astype(BF16)

    return start, wait, convert


def _stage_bf16(windows, stage_ref, sems):
    start, wait, convert = _window_stream(windows, stage_ref, sems)
    n_slots = stage_ref.shape[0]
    for j in range(n_slots):
        start(j)
    for j in range(len(windows)):
        wait(j)
        convert(j)
        start(j + n_slots)


def _row_windows(src, dst, n_rows, step):
    return [(src.at[pl.ds(r0, min(step, n_rows - r0)), :], dst.at[pl.ds(r0, min(step, n_rows - r0)), :])
            for r0 in range(0, n_rows, step)]


def _layer_spec(a, layer):
    nd = a.ndim
    return pl.BlockSpec((1,) + a.shape[1:], lambda *_: (layer,) + (0,) * (nd - 1),
                        pipeline_mode=pl.Buffered(1))


def _ada_kernel(c_ref, w_ref, b_ref, o_ref):
    c_act = _silu(c_ref[...]).astype(BF16)
    w = w_ref[0].astype(BF16)
    o_ref[0] = jnp.dot(c_act, w, preferred_element_type=F32) + b_ref[0]


def _ada_call(c_pad, w_ada, b_ada):
    L, D, N = w_ada.shape
    rows = c_pad.shape[0]
    return pl.pallas_call(
        _ada_kernel,
        grid=(L, N // ADA_TN),
        in_specs=[
            pl.BlockSpec((rows, D), lambda l, n: (0, 0)),
            pl.BlockSpec((1, D, ADA_TN), lambda l, n: (l, 0, n)),
            pl.BlockSpec((1, 1, ADA_TN), lambda l, n: (l, 0, n)),
        ],
        out_specs=pl.BlockSpec((1, rows, ADA_TN), lambda l, n: (l, 0, n)),
        out_shape=jax.ShapeDtypeStruct((L, rows, N), F32),
        compiler_params=pltpu.CompilerParams(
            dimension_semantics=("arbitrary", "arbitrary"),
            vmem_limit_bytes=VMEM_LIMIT),
        name="ada_mod",
    )(c_pad, w_ada, b_ada.reshape(L, 1, N))


def _ffn_kernel(xp_ref, x_ref, modp_ref, mod_ref, wup_hbm, wdn_hbm, lng_ref, lnb_ref,
                o_ref, a_ref, y_ref, wup_ref, wdn_ref, gstage_ref, ustage_ref, dstage_ref,
                g_sems, u_sems, d_sems, *, layer, mod_base, ln_row, n_tiles):
    i = pl.program_id(0)
    chunks = range(0, D_FF, FFN_TF)
    streams = [
        _window_stream([(wup_hbm.at[layer, :, pl.ds(c0, FFN_TF)], wup_ref.at[:, pl.ds(c0, FFN_TF)])
                        for c0 in chunks], gstage_ref, g_sems),
        _window_stream([(wup_hbm.at[layer, :, pl.ds(D_FF + c0, FFN_TF)],
                         wup_ref.at[:, pl.ds(D_FF + c0, FFN_TF)]) for c0 in chunks],
                       ustage_ref, u_sems),
        _window_stream(_row_windows(wdn_hbm.at[layer], wdn_ref, D_FF, FFN_TF), dstage_ref, d_sems),
    ]

    def post_norm(rb):
        rows = pl.ds(rb * LN_ROWS, LN_ROWS)
        gt = modp_ref[0, 0, mod_base + 2:mod_base + 3, :]
        t = ALPHA * xp_ref[rows, :] + (0.5 * (1.0 + gt)) * y_ref[rows, :]
        out = _layer_norm(t, lng_ref[0, ln_row:ln_row + 1, :], lnb_ref[0, ln_row:ln_row + 1, :])
        o_ref[rows, :] = out
        return out

    def tile(stage_weights):
        if stage_weights:
            for start, _, _ in streams:
                start(0)
                start(1)
        sh = mod_ref[0, 0, mod_base:mod_base + 1, :]
        sc = mod_ref[0, 0, mod_base + 1:mod_base + 2, :]
        h = (x_ref[...] * (1.0 + sc) + sh).astype(BF16)
        anchor = None
        for ci, c0 in enumerate(chunks):
            if stage_weights:
                for _, wait, _ in streams:
                    wait(ci)
                for start, _, convert in streams:
                    convert(ci)
                    start(ci + 2)
            g = jnp.dot(h, wup_ref[:, c0:c0 + FFN_TF], preferred_element_type=F32)
            u = jnp.dot(h, wup_ref[:, D_FF + c0:D_FF + c0 + FFN_TF], preferred_element_type=F32)
            act = _silu(g) * u
            a_ref[:, c0:c0 + FFN_TF] = act.astype(BF16)
            if anchor is not None:
                a_ref[0:BF16_ROWS, c0:c0 + LANES] = _after(act[0:BF16_ROWS, 0:LANES],
                                                           anchor).astype(BF16)
            anchor = _zero_bits_of(post_norm(ci)) if ci < FFN_TM // LN_ROWS else None
        y_ref[...] = jnp.dot(a_ref[...], wdn_ref[...], preferred_element_type=F32)

    @pl.when(i == 0)
    def _():
        y_ref[...] = jnp.zeros_like(y_ref)
        tile(stage_weights=True)

    @pl.when((i > 0) & (i < n_tiles))
    def _():
        tile(stage_weights=False)

    @pl.when(i == n_tiles)
    def _():
        for rb in range(FFN_TM // LN_ROWS):
            post_norm(rb)


def _ffn_call(x2, mod, w_up, w_down, ln_g, ln_b, *, layer, mod_base, ln_row, seq):
    M, D = x2.shape
    n_tiles = M // FFN_TM
    tiles_per_seq = seq // FFN_TM
    assert D_FF // FFN_TF >= FFN_TM // LN_ROWS

    def cur(i):
        return jnp.minimum(i, n_tiles - 1)

    def prev(i):
        return jnp.maximum(i - 1, 0)

    return pl.pallas_call(
        functools.partial(_ffn_kernel, layer=layer, mod_base=mod_base, ln_row=ln_row,
                          n_tiles=n_tiles),
        grid=(n_tiles + 1,),
        in_specs=[
            pl.BlockSpec((FFN_TM, D), lambda i: (prev(i), 0)),
            pl.BlockSpec((FFN_TM, D), lambda i: (cur(i), 0)),
            pl.BlockSpec((1, 1, N_MOD, D), lambda i: (layer, prev(i) // tiles_per_seq, 0, 0)),
            pl.BlockSpec((1, 1, N_MOD, D), lambda i: (layer, cur(i) // tiles_per_seq, 0, 0)),
            pl.BlockSpec(memory_space=pl.ANY),
            pl.BlockSpec(memory_space=pl.ANY),
            _layer_spec(ln_g, layer),
            _layer_spec(ln_b, layer),
        ],
        out_specs=pl.BlockSpec((FFN_TM, D), lambda i: (prev(i), 0)),
        out_shape=jax.ShapeDtypeStruct((M, D), F32),
        scratch_shapes=[
            pltpu.VMEM((FFN_TM, D_FF), BF16),
            pltpu.VMEM((FFN_TM, D), F32),
            pltpu.VMEM((D, 2 * D_FF), BF16),
            pltpu.VMEM((D_FF, D), BF16),
            pltpu.VMEM((2, D, FFN_TF), F32),
            pltpu.VMEM((2, D, FFN_TF), F32),
            pltpu.VMEM((2, FFN_TF, D), F32),
            pltpu.SemaphoreType.DMA((2,)),
            pltpu.SemaphoreType.DMA((2,)),
            pltpu.SemaphoreType.DMA((2,)),
        ],
        compiler_params=pltpu.CompilerParams(
            dimension_semantics=("arbitrary",),
            vmem_limit_bytes=VMEM_LIMIT),
        name="ffn",
    )(x2, x2, mod, mod, w_up, w_down, ln_g, ln_b)


def _split_bf16(t, parts):
    out = []
    r = t
    for _ in range(parts):
        p = r.astype(BF16)
        out.append(p)
        r = r - p.astype(F32)
    return out


def _mixer_tile(s, post_norm, x_ref, mod_ref, walpha_ref, balpha_ref, gng_ref, wpool_ref, pscale_ref,
                state_ref, uext_ref, qt_ref, kt_ref, kend_ref, v_ref, dect_ref, og_ref,
                rg_ref, gg_ref, gp_ref, wint_ref, wpg_ref, wpp_ref, wout_ref, y_ref):
    TS = MIX_TS
    NCH = TS // CHUNK

    @pl.when(s == 0)
    def _():
        state_ref[...] = jnp.zeros_like(state_ref)
        uext_ref[0:POOL_HALO, :] = jnp.zeros((POOL_HALO, POOL_WIDTH), F32)

    @pl.when(s > 0)
    def _():
        uext_ref[0:POOL_HALO, :] = uext_ref[TS:TS + POOL_HALO, :]

    x = x_ref[0]
    sh = mod_ref[0, 0, 3:4, :]
    sc = mod_ref[0, 0, 4:5, :]
    h = (x * (1.0 + sc) + sh).astype(BF16)

    nt = (((1,), (1,)), ((), ()))

    def proj(r0, r1):
        return lax.dot_general(h, wint_ref[r0:r1, :], nt, preferred_element_type=F32)

    a_wide = proj(IN_A, IN_A + LANES)
    v_ref[:, 0:PIECE] = proj(IN_V, IN_V + PIECE).astype(BF16)
    lane = lax.broadcasted_iota(jnp.int32, a_wide.shape, 1)
    a_lr = jnp.where(lane < GLA_RANK, a_wide, 0.0).astype(BF16)
    pre = jnp.dot(a_lr, walpha_ref[0], preferred_element_type=F32) + balpha_ref[0]
    log_a = (jnp.minimum(pre, 0.0) - jnp.log(1.0 + jnp.exp(-jnp.abs(pre)))) / GLA_TAU

    v_ref[:, PIECE:GLA_DV] = proj(IN_V + PIECE, IN_R).astype(BF16)

    ri = lax.broadcasted_iota(jnp.int32, (CUM_ROWS, CUM_ROWS), 0)
    ci = lax.broadcasted_iota(jnp.int32, (CUM_ROWS, CUM_ROWS), 1)
    tri = jnp.where((ri // CHUNK == ci // CHUNK) & (ci <= ri), 1.0, 0.0).astype(BF16)
    chunks_per_group = CUM_ROWS // CHUNK

    def group_decay(g):
        g0 = g * CUM_ROWS
        dec_rows = []
        b = None
        for part in _split_bf16(log_a[g0:g0 + CUM_ROWS], CUM_TERMS):
            t = jnp.dot(tri, part, preferred_element_type=F32)
            b = t if b is None else b + t
        h_g = h[g0:g0 + CUM_ROWS]
        q = lax.dot_general(h_g, wint_ref[IN_Q:IN_K, :], nt,
                            preferred_element_type=F32) * (GLA_HK ** -0.5)
        k = lax.dot_general(h_g, wint_ref[IN_K:IN_V, :], nt, preferred_element_type=F32)
        qt_ref[g0:g0 + CUM_ROWS, :] = (q * jnp.exp(b)).astype(BF16)
        kt_ref[g0:g0 + CUM_ROWS, :] = (k * jnp.exp(-b)).astype(BF16)
        for r0 in range(0, CUM_ROWS, CHUNK):
            b_c = b[r0:r0 + CHUNK]
            b_last = b_c[CHUNK - 1:CHUNK, :]
            kend_ref[g0 + r0:g0 + r0 + CHUNK, :] = (k[r0:r0 + CHUNK]
                                                    * jnp.exp(b_last - b_c)).astype(BF16)
            dec_rows.append(jnp.exp(b_last))
        dec_all = jnp.concatenate(
            dec_rows + [jnp.zeros((LANES - chunks_per_group, GLA_DK), F32)], axis=0)
        dect_ref[g] = dec_all.T

    group_decay(0)

    def r_piece(c0):
        rg_ref[:, c0:c0 + PIECE] = _silu(proj(IN_R + c0, IN_R + c0 + PIECE))

    def u_piece(c0):
        uext_ref[POOL_HALO:POOL_HALO + TS, c0:c0 + PIECE] = proj(IN_U + c0, IN_U + c0 + PIECE)

    def gp_piece(c0):
        gp_ref[:, c0:c0 + PIECE] = jax.nn.sigmoid(proj(IN_GP + c0, IN_GP + c0 + PIECE))

    def gg_piece(c0):
        gg_ref[:, c0:c0 + PIECE] = jax.nn.sigmoid(proj(IN_GG + c0, IN_GG + c0 + PIECE))

    pieces = ([functools.partial(r_piece, c0) for c0 in range(0, GLA_DV, PIECE)]
              + [functools.partial(u_piece, c0) for c0 in range(0, POOL_WIDTH, PIECE)]
              + [functools.partial(gp_piece, c0) for c0 in range(0, D_MODEL, PIECE)])

    row = lax.broadcasted_iota(jnp.int32, (CHUNK, CHUNK), 0)
    col = lax.broadcasted_iota(jnp.int32, (CHUNK, CHUNK), 1)
    causal = col <= row
    tn = (((0,), (0,)), ((), ()))
    n_steps = NCH * GLA_HEADS
    step = 0

    def chunk_scores(c):
        out = []
        for hd in range(GLA_HEADS):
            rows, cols = slice(c * CHUNK, (c + 1) * CHUNK), slice(hd * GLA_HK, (hd + 1) * GLA_HK)
            sc = lax.dot_general(qt_ref[rows, cols], kt_ref[rows, cols], nt,
                                 preferred_element_type=F32)
            out.append(jnp.where(causal, sc, 0.0).astype(BF16))
        return out

    for _ in range(PIECES_AHEAD):
        pieces.pop(0)()
    scores_next = chunk_scores(0)
    for c in range(NCH):
        r0 = c * CHUNK
        scores_cur = scores_next
        if c + 1 < NCH:
            scores_next = chunk_scores(c + 1)
        anchor = _zero_bits_of(post_norm(c)) if c < TS // LN_ROWS else None
        for hd in range(GLA_HEADS):
            k0 = hd * GLA_HK
            v0 = hd * GLA_HV
            q_c = qt_ref[r0:r0 + CHUNK, k0:k0 + GLA_HK]
            ke_c = kend_ref[r0:r0 + CHUNK, k0:k0 + GLA_HK]
            v_c = v_ref[r0:r0 + CHUNK, v0:v0 + GLA_HV]
            st = state_ref[hd]
            scores = scores_cur[hd]
            lhs = jnp.concatenate([q_c, scores], axis=1)
            rhs = jnp.concatenate([st.astype(BF16), v_c], axis=0)
            og_ref[r0:r0 + CHUNK, v0:v0 + GLA_HV] = jnp.dot(lhs, rhs, preferred_element_type=F32)
            kv = lax.dot_general(ke_c, v_c, tn, preferred_element_type=F32)
            cg, cj = divmod(c, chunks_per_group)
            st_new = st * dect_ref[cg, k0:k0 + GLA_HK, cj:cj + 1] + kv
            state_ref[hd] = st_new
            if anchor is not None and hd == GLA_HEADS - 1:
                state_ref[hd, 0:SUBLANES, 0:LANES] = _after(st_new[0:SUBLANES, 0:LANES], anchor)
            step += 1
            while pieces and len(pieces) * n_steps > (n_steps - step) * N_PIECES:
                pieces.pop(0)()
        if c % chunks_per_group == 0 and c // chunks_per_group + 1 < TS // CUM_ROWS:
            group_decay(c // chunks_per_group + 1)
    while pieces:
        pieces.pop(0)()

    o_parts = []
    for hd in range(GLA_HEADS):
        v0 = hd * GLA_HV
        o_h = og_ref[:, v0:v0 + GLA_HV]
        mu = jnp.mean(o_h, axis=-1, keepdims=True)
        dlt = o_h - mu
        var = jnp.mean(dlt * dlt, axis=-1, keepdims=True)
        o_parts.append((dlt * lax.rsqrt(var + LN_EPS) * gng_ref[0, :, v0:v0 + GLA_HV]
                        * rg_ref[:, v0:v0 + GLA_HV]).astype(BF16))
        gg_piece(hd * PIECE)
    o_n = jnp.concatenate(o_parts, axis=-1)
    y_gla = jnp.dot(o_n, wpg_ref[...], preferred_element_type=F32)

    pos = s * TS + lax.broadcasted_iota(jnp.int32, (TS, POOL_GW), 0)
    p_parts = []
    for gi, w in enumerate(POOL_WINDOWS):
        c0 = gi * POOL_GW
        ext = uext_ref[:, c0:c0 + POOL_GW]
        span = 1
        while span < w:
            ext = ext + pltpu.roll(ext, span, axis=0)
            span *= 2
        wsum = ext[POOL_HALO:POOL_HALO + TS]
        u_g = uext_ref[POOL_HALO:POOL_HALO + TS, c0:c0 + POOL_GW]
        cnt = jnp.minimum(pos + 1, w).astype(F32)
        p_g = (wsum / cnt - u_g).astype(BF16)
        p_parts.append(jnp.dot(p_g, wpool_ref[0, gi], preferred_element_type=F32))
    p = (jnp.concatenate(p_parts, axis=-1) * pscale_ref[0]).astype(BF16)
    y_pool = jnp.dot(p, wpp_ref[...], preferred_element_type=F32)

    merged = gg_ref[...] * y_gla + gp_ref[...] * y_pool
    y_ref[...] = jnp.dot(merged.astype(BF16), wout_ref[...], preferred_element_type=F32)


def _mixer_kernel(xp_ref, x_ref, modp_ref, mod_ref, wint_hbm, wpg_hbm, wpp_hbm, wout_hbm,
                  walpha_ref, balpha_ref, gng_ref, wpool_ref, pscale_ref, lng_ref, lnb_ref,
                  o_ref,
                  state_ref, uext_ref, qt_ref, kt_ref, kend_ref, v_ref, dect_ref, og_ref,
                  rg_ref, gg_ref, gp_ref,
                  wint_ref, wpg_ref, wpp_ref, wout_ref, y_ref, stage_ref, sems,
                  *, layer, n_tiles, tiles_per_seq):
    i = pl.program_id(0)

    def post_norm(rb):
        rows = pl.ds(rb * LN_ROWS, LN_ROWS)
        gt = modp_ref[0, 0, 5:6, :]
        t = ALPHA * xp_ref[0, rows, :] + (1.0 + gt) * y_ref[rows, :]
        out = _layer_norm(t, lng_ref[0, 1:2, :], lnb_ref[0, 1:2, :])
        o_ref[0, rows, :] = out
        return out

    @pl.when(i == 0)
    def _():
        y_ref[...] = jnp.zeros_like(y_ref)
        _stage_bf16(
            _row_windows(wint_hbm.at[layer], wint_ref, IN_WIDTH, RING_ROWS)
            + _row_windows(wpg_hbm.at[layer], wpg_ref, GLA_DV, RING_ROWS)
            + _row_windows(wpp_hbm.at[layer], wpp_ref, POOL_WIDTH, RING_ROWS)
            + _row_windows(wout_hbm.at[layer], wout_ref, D_MODEL, RING_ROWS),
            stage_ref, sems)

    @pl.when(i < n_tiles)
    def _():
        _mixer_tile(i % tiles_per_seq, post_norm, x_ref, mod_ref, walpha_ref, balpha_ref, gng_ref,
                    wpool_ref, pscale_ref, state_ref, uext_ref, qt_ref, kt_ref, kend_ref, v_ref,
                    dect_ref, og_ref, rg_ref, gg_ref, gp_ref, wint_ref, wpg_ref, wpp_ref, wout_ref,
                    y_ref)

    @pl.when(i == n_tiles)
    def _():
        for rb in range(MIX_TS // LN_ROWS):
            post_norm(rb)


def _mixer_call(x, mod, big_weights, small_operands, *, layer):
    B, S, D = x.shape
    TS = MIX_TS
    tiles_per_seq = S // TS
    n_tiles = B * tiles_per_seq

    def cur(i):
        return jnp.minimum(i, n_tiles - 1)

    def prev(i):
        return jnp.maximum(i - 1, 0)

    def x_map(tile):
        return lambda i: (tile(i) // tiles_per_seq, tile(i) % tiles_per_seq, 0)

    def mod_map(tile):
        return lambda i: (layer, tile(i) // tiles_per_seq, 0, 0)

    return pl.pallas_call(
        functools.partial(_mixer_kernel, layer=layer, n_tiles=n_tiles, tiles_per_seq=tiles_per_seq),
        grid=(n_tiles + 1,),
        in_specs=[
            pl.BlockSpec((1, TS, D), x_map(prev)),
            pl.BlockSpec((1, TS, D), x_map(cur)),
            pl.BlockSpec((1, 1, N_MOD, D), mod_map(prev)),
            pl.BlockSpec((1, 1, N_MOD, D), mod_map(cur)),
        ] + [pl.BlockSpec(memory_space=pl.ANY) for _ in big_weights]
          + [_layer_spec(a, layer) for a in small_operands],
        out_specs=pl.BlockSpec((1, TS, D), x_map(prev)),
        out_shape=jax.ShapeDtypeStruct((B, S, D), F32),
        scratch_shapes=[
            pltpu.VMEM((GLA_HEADS, GLA_HK, GLA_HV), F32),
            pltpu.VMEM((POOL_HALO + TS, POOL_WIDTH), F32),
            pltpu.VMEM((TS, GLA_DK), BF16),
            pltpu.VMEM((TS, GLA_DK), BF16),
            pltpu.VMEM((TS, GLA_DK), BF16),
            pltpu.VMEM((TS, GLA_DV), BF16),
            pltpu.VMEM((TS // CUM_ROWS, GLA_DK, LANES), F32),
            pltpu.VMEM((TS, GLA_DV), F32),
            pltpu.VMEM((TS, GLA_DV), F32),
            pltpu.VMEM((TS, D_MODEL), F32),
            pltpu.VMEM((TS, D_MODEL), F32),
            pltpu.VMEM((IN_WIDTH, D_MODEL), BF16),
            pltpu.VMEM((GLA_DV, D_MODEL), BF16),
            pltpu.VMEM((POOL_WIDTH, D_MODEL), BF16),
            pltpu.VMEM((D_MODEL, D_MODEL), BF16),
            pltpu.VMEM((TS, D_MODEL), F32),
            pltpu.VMEM((RING_SLOTS, RING_ROWS, D_MODEL), F32),
            pltpu.SemaphoreType.DMA((RING_SLOTS,)),
        ],
        compiler_params=pltpu.CompilerParams(
            dimension_semantics=("arbitrary",),
            vmem_limit_bytes=VMEM_LIMIT),
        name="mixer",
    )(x, x, mod, mod, *big_weights, *small_operands)


def _mixer_slab_operands(w_alpha, b_alpha, gla_norm_g, w_pool, pool_scale, ln_g, ln_b):
    L = w_alpha.shape[0]
    w_al = jnp.pad(w_alpha, ((0, 0), (0, LANES - GLA_RANK), (0, 0))).astype(BF16)
    return (w_al, b_alpha.reshape(L, 1, GLA_DK), gla_norm_g.reshape(L, 1, GLA_DV),
            w_pool.astype(BF16), pool_scale.reshape(L, 1, POOL_WIDTH), ln_g, ln_b)


def kernel(x, c, w_ada, b_ada, ffn1_up, ffn1_down, w_in, w_alpha, b_alpha, gla_norm_g,
           w_proj_gla, w_pool, pool_scale, w_proj_pool, w_out, ffn2_up, ffn2_down, ln_g, ln_b):
    B, S, D = x.shape
    L = w_ada.shape[0]
    c_pad = jnp.pad(c, ((0, SUBLANES - B), (0, 0)))
    mod = _ada_call(c_pad, w_ada, b_ada)[:, :B].reshape(L, B, N_MOD, D)
    mix_big = (jnp.swapaxes(w_in, 1, 2), w_proj_gla, w_proj_pool, w_out)
    mix_small = _mixer_slab_operands(w_alpha, b_alpha, gla_norm_g, w_pool, pool_scale, ln_g, ln_b)
    x2 = x.reshape(B * S, D)
    for l in range(L):
        x2 = _ffn_call(x2, mod, ffn1_up, ffn1_down, ln_g, ln_b, layer=l, mod_base=0, ln_row=0,
                       seq=S)
        x3 = _mixer_call(x2.reshape(B, S, D), mod, mix_big, mix_small, layer=l)
        x2 = _ffn_call(x3.reshape(B * S, D), mod, ffn2_up, ffn2_down, ln_g, ln_b, layer=l,
                       mod_base=6, ln_row=2, seq=S)
    return x2.reshape(B, S, D)
```

```python
import functools

import jax
import jax.numpy as jnp
from jax import lax
from jax.experimental import pallas as pl
from jax.experimental.pallas import tpu as pltpu

F32 = jnp.float32
BF16 = jnp.bfloat16

D_MODEL = 1024
DEPTH = 4
GLA_HEADS = 4
GLA_DK = 512
GLA_DV = 1024
GLA_HK = 128
GLA_HV = 256
GLA_RANK = 16
GLA_TAU = 16.0
CHUNK = 64
POOL_WIDTH = 512
POOL_WINDOWS = (2, 4, 8, 16)
POOL_GW = 128
POOL_HALO = 16
D_FF = 2816
N_MOD = 9
ALPHA = (2 * DEPTH) ** 0.25
LN_EPS = 1e-5
LANES = 128
SUBLANES = 8
BF16_ROWS = 16
V7X_VMEM_BYTES = 64 * 1024 * 1024

FFN_TM = 512
FFN_TF = 256
MIX_TS = 512
LN_ROWS = 64
RING_ROWS = 512
RING_SLOTS = 3
CUM_ROWS = 256
ADA_TN = 2304
VMEM_LIMIT = V7X_VMEM_BYTES * 7 // 8

IN_Q = 0
IN_K = IN_Q + GLA_DK
IN_V = IN_K + GLA_DK
IN_R = IN_V + GLA_DV
IN_A = IN_R + GLA_DV
IN_U = IN_A + GLA_RANK
IN_GG = IN_U + POOL_WIDTH
IN_GP = IN_GG + D_MODEL
IN_WIDTH = IN_GP + D_MODEL
PIECE = 256
PIECES_AHEAD = 2
N_PIECES = (GLA_DV + POOL_WIDTH + D_MODEL) // PIECE
CUM_TERMS = 2


def _layer_norm(t, g, b):
    mu = jnp.mean(t, axis=-1, keepdims=True)
    d = t - mu
    var = jnp.mean(d * d, axis=-1, keepdims=True)
    return d * lax.rsqrt(var + LN_EPS) * g + b


def _silu(t):
    return t * jax.nn.sigmoid(t)


def _zero_bits_of(v):
    bits = lax.bitcast_convert_type(v, jnp.uint32)
    acc = None
    for r0 in range(0, v.shape[0], SUBLANES):
        for c0 in range(0, v.shape[1], LANES):
            tile = bits[r0:r0 + SUBLANES, c0:c0 + LANES]
            acc = tile if acc is None else acc | tile
    return lax.shift_right_logical(lax.shift_right_logical(acc, jnp.uint32(16)), jnp.uint32(16))


def _after(v, zero_bits):
    reps = v.shape[0] // zero_bits.shape[0]
    z = jnp.concatenate([zero_bits] * reps, axis=0) if reps > 1 else zero_bits
    return jnp.where(z == 0, v, 0.0)


def _window_stream(windows, stage_ref, sems):
    n_slots = stage_ref.shape[0]

    def copy(j):
        src, _ = windows[j]
        slot = stage_ref.at[j % n_slots, 0:src.shape[0], 0:src.shape[1]]
        return pltpu.make_async_copy(src, slot, sems.at[j % n_slots])

    def start(j):
        if j < len(windows):
            copy(j).start()

    def wait(j):
        copy(j).wait()

    def convert(j):
        src, dst = windows[j]
        dst[...] = stage_ref[j % n_slots, 0:src.shape[0], 0:src.shape[1]].astype(BF16)

    return start, wait, convert


def _stage_bf16(windows, stage_ref, sems):
    start, wait, convert = _window_stream(windows, stage_ref, sems)
    n_slots = stage_ref.shape[0]
    for j in range(n_slots):
        start(j)
    for j in range(len(windows)):
        wait(j)
        convert(j)
        start(j + n_slots)


def _row_windows(src, dst, n_rows, step):
    return [(src.at[pl.ds(r0, min(step, n_rows - r0)), :], dst.at[pl.ds(r0, min(step, n_rows - r0)), :])
            for r0 in range(0, n_rows, step)]


def _layer_spec(a, layer):
    nd = a.ndim
    return pl.BlockSpec((1,) + a.shape[1:], lambda *_: (layer,) + (0,) * (nd - 1),
                        pipeline_mode=pl.Buffered(1))


def _ada_kernel(c_ref, w_ref, b_ref, o_ref):
    c_act = _silu(c_ref[...]).astype(BF16)
    w = w_ref[0].astype(BF16)
    o_ref[0] = jnp.dot(c_act, w, preferred_element_type=F32) + b_ref[0]


def _ada_call(c_pad, w_ada, b_ada):
    L, D, N = w_ada.shape
    rows = c_pad.shape[0]
    return pl.pallas_call(
        _ada_kernel,
        grid=(L, N // ADA_TN),
        in_specs=[
            pl.BlockSpec((rows, D), lambda l, n: (0, 0)),
            pl.BlockSpec((1, D, ADA_TN), lambda l, n: (l, 0, n)),
            pl.BlockSpec((1, 1, ADA_TN), lambda l, n: (l, 0, n)),
        ],
        out_specs=pl.BlockSpec((1, rows, ADA_TN), lambda l, n: (l, 0, n)),
        out_shape=jax.ShapeDtypeStruct((L, rows, N), F32),
        compiler_params=pltpu.CompilerParams(
            dimension_semantics=("arbitrary", "arbitrary"),
            vmem_limit_bytes=VMEM_LIMIT),
        name="ada_mod",
    )(c_pad, w_ada, b_ada.reshape(L, 1, N))


def _ffn_kernel(xp_ref, x_ref, modp_ref, mod_ref, wup_hbm, wdn_hbm, lng_ref, lnb_ref,
                o_ref, a_ref, y_ref, wup_ref, wdn_ref, gstage_ref, ustage_ref, dstage_ref,
                g_sems, u_sems, d_sems, *, layer, mod_base, ln_row, n_tiles):
    i = pl.program_id(0)
    chunks = range(0, D_FF, FFN_TF)
    streams = [
        _window_stream([(wup_hbm.at[layer, :, pl.ds(c0, FFN_TF)], wup_ref.at[:, pl.ds(c0, FFN_TF)])
                        for c0 in chunks], gstage_ref, g_sems),
        _window_stream([(wup_hbm.at[layer, :, pl.ds(D_FF + c0, FFN_TF)],
                         wup_ref.at[:, pl.ds(D_FF + c0, FFN_TF)]) for c0 in chunks],
                       ustage_ref, u_sems),
        _window_stream(_row_windows(wdn_hbm.at[layer], wdn_ref, D_FF, FFN_TF), dstage_ref, d_sems),
    ]

    def post_norm(rb):
        rows = pl.ds(rb * LN_ROWS, LN_ROWS)
        gt = modp_ref[0, 0, mod_base + 2:mod_base + 3, :]
        t = ALPHA * xp_ref[rows, :] + (0.5 * (1.0 + gt)) * y_ref[rows, :]
        out = _layer_norm(t, lng_ref[0, ln_row:ln_row + 1, :], lnb_ref[0, ln_row:ln_row + 1, :])
        o_ref[rows, :] = out
        return out

    def tile(stage_weights):
        if stage_weights:
            for start, _, _ in streams:
                start(0)
                start(1)
        sh = mod_ref[0, 0, mod_base:mod_base + 1, :]
        sc = mod_ref[0, 0, mod_base + 1:mod_base + 2, :]
        h = (x_ref[...] * (1.0 + sc) + sh).astype(BF16)
        anchor = None
        for ci, c0 in enumerate(chunks):
            if stage_weights:
                for _, wait, _ in streams:
                    wait(ci)
                for start, _, convert in streams:
                    convert(ci)
                    start(ci + 2)
            g = jnp.dot(h, wup_ref[:, c0:c0 + FFN_TF], preferred_element_type=F32)
            u = jnp.dot(h, wup_ref[:, D_FF + c0:D_FF + c0 + FFN_TF], preferred_element_type=F32)
            act = _silu(g) * u
            a_ref[:, c0:c0 + FFN_TF] = act.astype(BF16)
            if anchor is not None:
                a_ref[0:BF16_ROWS, c0:c0 + LANES] = _after(act[0:BF16_ROWS, 0:LANES],
                                                           anchor).astype(BF16)
            anchor = _zero_bits_of(post_norm(ci)) if ci < FFN_TM // LN_ROWS else None
        y_ref[...] = jnp.dot(a_ref[...], wdn_ref[...], preferred_element_type=F32)

    @pl.when(i == 0)
    def _():
        y_ref[...] = jnp.zeros_like(y_ref)
        tile(stage_weights=True)

    @pl.when((i > 0) & (i < n_tiles))
    def _():
        tile(stage_weights=False)

    @pl.when(i == n_tiles)
    def _():
        for rb in range(FFN_TM // LN_ROWS):
            post_norm(rb)


def _ffn_call(x2, mod, w_up, w_down, ln_g, ln_b, *, layer, mod_base, ln_row, seq):
    M, D = x2.shape
    n_tiles = M // FFN_TM
    tiles_per_seq = seq // FFN_TM
    assert D_FF // FFN_TF >= FFN_TM // LN_ROWS

    def cur(i):
        return jnp.minimum(i, n_tiles - 1)

    def prev(i):
        return jnp.maximum(i - 1, 0)

    return pl.pallas_call(
        functools.partial(_ffn_kernel, layer=layer, mod_base=mod_base, ln_row=ln_row,
                          n_tiles=n_tiles),
        grid=(n_tiles + 1,),
        in_specs=[
            pl.BlockSpec((FFN_TM, D), lambda i: (prev(i), 0)),
            pl.BlockSpec((FFN_TM, D), lambda i: (cur(i), 0)),
            pl.BlockSpec((1, 1, N_MOD, D), lambda i: (layer, prev(i) // tiles_per_seq, 0, 0)),
            pl.BlockSpec((1, 1, N_MOD, D), lambda i: (layer, cur(i) // tiles_per_seq, 0, 0)),
            pl.BlockSpec(memory_space=pl.ANY),
            pl.BlockSpec(memory_space=pl.ANY),
            _layer_spec(ln_g, layer),
            _layer_spec(ln_b, layer),
        ],
        out_specs=pl.BlockSpec((FFN_TM, D), lambda i: (prev(i), 0)),
        out_shape=jax.ShapeDtypeStruct((M, D), F32),
        scratch_shapes=[
            pltpu.VMEM((FFN_TM, D_FF), BF16),
            pltpu.VMEM((FFN_TM, D), F32),
            pltpu.VMEM((D, 2 * D_FF), BF16),
            pltpu.VMEM((D_FF, D), BF16),
            pltpu.VMEM((2, D, FFN_TF), F32),
            pltpu.VMEM((2, D, FFN_TF), F32),
            pltpu.VMEM((2, FFN_TF, D), F32),
            pltpu.SemaphoreType.DMA((2,)),
            pltpu.SemaphoreType.DMA((2,)),
            pltpu.SemaphoreType.DMA((2,)),
        ],
        compiler_params=pltpu.CompilerParams(
            dimension_semantics=("arbitrary",),
            vmem_limit_bytes=VMEM_LIMIT),
        name="ffn",
    )(x2, x2, mod, mod, w_up, w_down, ln_g, ln_b)


def _split_bf16(t, parts):
    out = []
    r = t
    for _ in range(parts):
        p = r.astype(BF16)
        out.append(p)
        r = r - p.astype(F32)
    return out


def _mixer_tile(s, post_norm, x_ref, mod_ref, walpha_ref, balpha_ref, gng_ref, wpool_ref, pscale_ref,
                state_ref, uext_ref, qt_ref, kt_ref, kend_ref, v_ref, dect_ref, og_ref,
                rg_ref, gg_ref, gp_ref, wint_ref, wpg_ref, wpp_ref, wout_ref, y_ref):
    TS = MIX_TS
    NCH = TS // CHUNK

    @pl.when(s == 0)
    def _():
        state_ref[...] = jnp.zeros_like(state_ref)
        uext_ref[0:POOL_HALO, :] = jnp.zeros((POOL_HALO, POOL_WIDTH), F32)

    @pl.when(s > 0)
    def _():
        uext_ref[0:POOL_HALO, :] = uext_ref[TS:TS + POOL_HALO, :]

    x = x_ref[0]
    sh = mod_ref[0, 0, 3:4, :]
    sc = mod_ref[0, 0, 4:5, :]
    h = (x * (1.0 + sc) + sh).astype(BF16)

    nt = (((1,), (1,)), ((), ()))

    def proj(r0, r1):
        return lax.dot_general(h, wint_ref[r0:r1, :], nt, preferred_element_type=F32)

    a_wide = proj(IN_A, IN_A + LANES)
    v_ref[:, 0:PIECE] = proj(IN_V, IN_V + PIECE).astype(BF16)
    lane = lax.broadcasted_iota(jnp.int32, a_wide.shape, 1)
    a_lr = jnp.where(lane < GLA_RANK, a_wide, 0.0).astype(BF16)
    pre = jnp.dot(a_lr, walpha_ref[0], preferred_element_type=F32) + balpha_ref[0]
    log_a = (jnp.minimum(pre, 0.0) - jnp.log(1.0 + jnp.exp(-jnp.abs(pre)))) / GLA_TAU

    v_ref[:, PIECE:GLA_DV] = proj(IN_V + PIECE, IN_R).astype(BF16)

    ri = lax.broadcasted_iota(jnp.int32, (CUM_ROWS, CUM_ROWS), 0)
    ci = lax.broadcasted_iota(jnp.int32, (CUM_ROWS, CUM_ROWS), 1)
    tri = jnp.where((ri // CHUNK == ci // CHUNK) & (ci <= ri), 1.0, 0.0).astype(BF16)
    b_parts = []
    for g0 in range(0, TS, CUM_ROWS):
        acc = None
        for part in _split_bf16(log_a[g0:g0 + CUM_ROWS], CUM_TERMS):
            t = jnp.dot(tri, part, preferred_element_type=F32)
            acc = t if acc is None else acc + t
        b_parts.append(acc)
    b = jnp.concatenate(b_parts, axis=0)

    q = proj(IN_Q, IN_K) * (GLA_HK ** -0.5)
    k = proj(IN_K, IN_V)
    qt_ref[...] = (q * jnp.exp(b)).astype(BF16)
    k_t = k * jnp.exp(-b)
    kt_ref[...] = k_t.astype(BF16)
    dec_rows = []
    for c in range(NCH):
        r0 = c * CHUNK
        dec_c = jnp.exp(b[r0 + CHUNK - 1:r0 + CHUNK, :])
        kend_ref[r0:r0 + CHUNK, :] = (k_t[r0:r0 + CHUNK] * dec_c).astype(BF16)
        dec_rows.append(dec_c)
    dec_all = jnp.concatenate(
        dec_rows + [jnp.zeros((LANES - NCH, GLA_DK), F32)], axis=0)
    dect_ref[...] = dec_all.T

    def r_piece(c0):
        rg_ref[:, c0:c0 + PIECE] = _silu(proj(IN_R + c0, IN_R + c0 + PIECE))

    def u_piece(c0):
        uext_ref[POOL_HALO:POOL_HALO + TS, c0:c0 + PIECE] = proj(IN_U + c0, IN_U + c0 + PIECE)

    def gp_piece(c0):
        gp_ref[:, c0:c0 + PIECE] = jax.nn.sigmoid(proj(IN_GP + c0, IN_GP + c0 + PIECE))

    def gg_piece(c0):
        gg_ref[:, c0:c0 + PIECE] = jax.nn.sigmoid(proj(IN_GG + c0, IN_GG + c0 + PIECE))

    pieces = ([functools.partial(r_piece, c0) for c0 in range(0, GLA_DV, PIECE)]
              + [functools.partial(u_piece, c0) for c0 in range(0, POOL_WIDTH, PIECE)]
              + [functools.partial(gp_piece, c0) for c0 in range(0, D_MODEL, PIECE)])

    row = lax.broadcasted_iota(jnp.int32, (CHUNK, CHUNK), 0)
    col = lax.broadcasted_iota(jnp.int32, (CHUNK, CHUNK), 1)
    causal = col <= row
    tn = (((0,), (0,)), ((), ()))
    n_steps = NCH * GLA_HEADS
    step = 0

    def chunk_scores(c):
        out = []
        for hd in range(GLA_HEADS):
            rows, cols = slice(c * CHUNK, (c + 1) * CHUNK), slice(hd * GLA_HK, (hd + 1) * GLA_HK)
            sc = lax.dot_general(qt_ref[rows, cols], kt_ref[rows, cols], nt,
                                 preferred_element_type=F32)
            out.append(jnp.where(causal, sc, 0.0).astype(BF16))
        return out

    for _ in range(PIECES_AHEAD):
        pieces.pop(0)()
    scores_next = chunk_scores(0)
    for c in range(NCH):
        r0 = c * CHUNK
        scores_cur = scores_next
        if c + 1 < NCH:
            scores_next = chunk_scores(c + 1)
        anchor = _zero_bits_of(post_norm(c)) if c < TS // LN_ROWS else None
        for hd in range(GLA_HEADS):
            k0 = hd * GLA_HK
            v0 = hd * GLA_HV
            q_c = qt_ref[r0:r0 + CHUNK, k0:k0 + GLA_HK]
            ke_c = kend_ref[r0:r0 + CHUNK, k0:k0 + GLA_HK]
            v_c = v_ref[r0:r0 + CHUNK, v0:v0 + GLA_HV]
            st = state_ref[hd]
            scores = scores_cur[hd]
            lhs = jnp.concatenate([q_c, scores], axis=1)
            rhs = jnp.concatenate([st.astype(BF16), v_c], axis=0)
            og_ref[r0:r0 + CHUNK, v0:v0 + GLA_HV] = jnp.dot(lhs, rhs, preferred_element_type=F32)
            kv = lax.dot_general(ke_c, v_c, tn, preferred_element_type=F32)
            st_new = st * dect_ref[k0:k0 + GLA_HK, c:c + 1] + kv
            state_ref[hd] = st_new
            if anchor is not None and hd == GLA_HEADS - 1:
                state_ref[hd, 0:SUBLANES, 0:LANES] = _after(st_new[0:SUBLANES, 0:LANES], anchor)
            step += 1
            while pieces and len(pieces) * n_steps > (n_steps - step) * N_PIECES:
                pieces.pop(0)()
    while pieces:
        pieces.pop(0)()

    o_parts = []
    for hd in range(GLA_HEADS):
        v0 = hd * GLA_HV
        o_h = og_ref[:, v0:v0 + GLA_HV]
        mu = jnp.mean(o_h, axis=-1, keepdims=True)
        dlt = o_h - mu
        var = jnp.mean(dlt * dlt, axis=-1, keepdims=True)
        o_parts.append((dlt * lax.rsqrt(var + LN_EPS) * gng_ref[0, :, v0:v0 + GLA_HV]
                        * rg_ref[:, v0:v0 + GLA_HV]).astype(BF16))
        gg_piece(hd * PIECE)
    o_n = jnp.concatenate(o_parts, axis=-1)
    y_gla = jnp.dot(o_n, wpg_ref[...], preferred_element_type=F32)

    pos = s * TS + lax.broadcasted_iota(jnp.int32, (TS, POOL_GW), 0)
    p_parts = []
    for gi, w in enumerate(POOL_WINDOWS):
        c0 = gi * POOL_GW
        ext = uext_ref[:, c0:c0 + POOL_GW]
        span = 1
        while span < w:
            ext = ext + pltpu.roll(ext, span, axis=0)
            span *= 2
        wsum = ext[POOL_HALO:POOL_HALO + TS]
        u_g = uext_ref[POOL_HALO:POOL_HALO + TS, c0:c0 + POOL_GW]
        cnt = jnp.minimum(pos + 1, w).astype(F32)
        p_g = (wsum / cnt - u_g).astype(BF16)
        p_parts.append(jnp.dot(p_g, wpool_ref[0, gi], preferred_element_type=F32))
    p = (jnp.concatenate(p_parts, axis=-1) * pscale_ref[0]).astype(BF16)
    y_pool = jnp.dot(p, wpp_ref[...], preferred_element_type=F32)

    merged = gg_ref[...] * y_gla + gp_ref[...] * y_pool
    y_ref[...] = jnp.dot(merged.astype(BF16), wout_ref[...], preferred_element_type=F32)


def _mixer_kernel(xp_ref, x_ref, modp_ref, mod_ref, wint_hbm, wpg_hbm, wpp_hbm, wout_hbm,
                  walpha_ref, balpha_ref, gng_ref, wpool_ref, pscale_ref, lng_ref, lnb_ref,
                  o_ref,
                  state_ref, uext_ref, qt_ref, kt_ref, kend_ref, v_ref, dect_ref, og_ref,
                  rg_ref, gg_ref, gp_ref,
                  wint_ref, wpg_ref, wpp_ref, wout_ref, y_ref, stage_ref, sems,
                  *, layer, n_tiles, tiles_per_seq):
    i = pl.program_id(0)

    def post_norm(rb):
        rows = pl.ds(rb * LN_ROWS, LN_ROWS)
        gt = modp_ref[0, 0, 5:6, :]
        t = ALPHA * xp_ref[0, rows, :] + (1.0 + gt) * y_ref[rows, :]
        out = _layer_norm(t, lng_ref[0, 1:2, :], lnb_ref[0, 1:2, :])
        o_ref[0, rows, :] = out
        return out

    @pl.when(i == 0)
    def _():
        y_ref[...] = jnp.zeros_like(y_ref)
        _stage_bf16(
            _row_windows(wint_hbm.at[layer], wint_ref, IN_WIDTH, RING_ROWS)
            + _row_windows(wpg_hbm.at[layer], wpg_ref, GLA_DV, RING_ROWS)
            + _row_windows(wpp_hbm.at[layer], wpp_ref, POOL_WIDTH, RING_ROWS)
            + _row_windows(wout_hbm.at[layer], wout_ref, D_MODEL, RING_ROWS),
            stage_ref, sems)

    @pl.when(i < n_tiles)
    def _():
        _mixer_tile(i % tiles_per_seq, post_norm, x_ref, mod_ref, walpha_ref, balpha_ref, gng_ref,
                    wpool_ref, pscale_ref, state_ref, uext_ref, qt_ref, kt_ref, kend_ref, v_ref,
                    dect_ref, og_ref, rg_ref, gg_ref, gp_ref, wint_ref, wpg_ref, wpp_ref, wout_ref,
                    y_ref)

    @pl.when(i == n_tiles)
    def _():
        for rb in range(MIX_TS // LN_ROWS):
            post_norm(rb)


def _mixer_call(x, mod, big_weights, small_operands, *, layer):
    B, S, D = x.shape
    TS = MIX_TS
    tiles_per_seq = S // TS
    n_tiles = B * tiles_per_seq

    def cur(i):
        return jnp.minimum(i, n_tiles - 1)

    def prev(i):
        return jnp.maximum(i - 1, 0)

    def x_map(tile):
        return lambda i: (tile(i) // tiles_per_seq, tile(i) % tiles_per_seq, 0)

    def mod_map(tile):
        return lambda i: (layer, tile(i) // tiles_per_seq, 0, 0)

    return pl.pallas_call(
        functools.partial(_mixer_kernel, layer=layer, n_tiles=n_tiles, tiles_per_seq=tiles_per_seq),
        grid=(n_tiles + 1,),
        in_specs=[
            pl.BlockSpec((1, TS, D), x_map(prev)),
            pl.BlockSpec((1, TS, D), x_map(cur)),
            pl.BlockSpec((1, 1, N_MOD, D), mod_map(prev)),
            pl.BlockSpec((1, 1, N_MOD, D), mod_map(cur)),
        ] + [pl.BlockSpec(memory_space=pl.ANY) for _ in big_weights]
          + [_layer_spec(a, layer) for a in small_operands],
        out_specs=pl.BlockSpec((1, TS, D), x_map(prev)),
        out_shape=jax.ShapeDtypeStruct((B, S, D), F32),
        scratch_shapes=[
            pltpu.VMEM((GLA_HEADS, GLA_HK, GLA_HV), F32),
            pltpu.VMEM((POOL_HALO + TS, POOL_WIDTH), F32),
            pltpu.VMEM((TS, GLA_DK), BF16),
            pltpu.VMEM((TS, GLA_DK), BF16),
            pltpu.VMEM((TS, GLA_DK), BF16),
            pltpu.VMEM((TS, GLA_DV), BF16),
            pltpu.VMEM((GLA_DK, LANES), F32),
            pltpu.VMEM((TS, GLA_DV), F32),
            pltpu.VMEM((TS, GLA_DV), F32),
            pltpu.VMEM((TS, D_MODEL), F32),
            pltpu.VMEM((TS, D_MODEL), F32),
            pltpu.VMEM((IN_WIDTH, D_MODEL), BF16),
            pltpu.VMEM((GLA_DV, D_MODEL), BF16),
            pltpu.VMEM((POOL_WIDTH, D_MODEL), BF16),
            pltpu.VMEM((D_MODEL, D_MODEL), BF16),
            pltpu.VMEM((TS, D_MODEL), F32),
            pltpu.VMEM((RING_SLOTS, RING_ROWS, D_MODEL), F32),
            pltpu.SemaphoreType.DMA((RING_SLOTS,)),
        ],
        compiler_params=pltpu.CompilerParams(
            dimension_semantics=("arbitrary",),
            vmem_limit_bytes=VMEM_LIMIT),
        name="mixer",
    )(x, x, mod, mod, *big_weights, *small_operands)


def _mixer_slab_operands(w_alpha, b_alpha, gla_norm_g, w_pool, pool_scale, ln_g, ln_b):
    L = w_alpha.shape[0]
    w_al = jnp.pad(w_alpha, ((0, 0), (0, LANES - GLA_RANK), (0, 0))).astype(BF16)
    return (w_al, b_alpha.reshape(L, 1, GLA_DK), gla_norm_g.reshape(L, 1, GLA_DV),
            w_pool.astype(BF16), pool_scale.reshape(L, 1, POOL_WIDTH), ln_g, ln_b)


def kernel(x, c, w_ada, b_ada, ffn1_up, ffn1_down, w_in, w_alpha, b_alpha, gla_norm_g,
           w_proj_gla, w_pool, pool_scale, w_proj_pool, w_out, ffn2_up, ffn2_down, ln_g, ln_b):
    B, S, D = x.shape
    L = w_ada.shape[0]
    c_pad = jnp.pad(c, ((0, SUBLANES - B), (0, 0)))
    mod = _ada_call(c_pad, w_ada, b_ada)[:, :B].reshape(L, B, N_MOD, D)
    mix_big = (jnp.swapaxes(w_in, 1, 2), w_proj_gla, w_proj_pool, w_out)
    mix_small = _mixer_slab_operands(w_alpha, b_alpha, gla_norm_g, w_pool, pool_scale, ln_g, ln_b)
    x2 = x.reshape(B * S, D)
    for l in range(L):
        x2 = _ffn_call(x2, mod, ffn1_up, ffn1_down, ln_g, ln_b, layer=l, mod_base=0, ln_row=0,
                       seq=S)
        x3 = _mixer_call(x2.reshape(B, S, D), mod, mix_big, mix_small, layer=l)
        x2 = _ffn_call(x3.reshape(B * S, D), mod, ffn2_up, ffn2_down, ln_g, ln_b, layer=l,
                       mod_base=6, ln_row=2, seq=S)
    return x2.reshape(B, S, D)
```
